```python
import math
import jax, jax.numpy as jnp
from jax import lax
import numpy as np

D_MODEL = 1024
BATCH = 16
SEQ = 2048
DEPTH = 4

GRID_W = 64
CTX_LEN = 256
N_MIXERS = 3
Q_BLOCK = 128
ROPE_THETA = 10000.0
LN_EPS = 1e-5
RMS_EPS = 1e-6
D_FF = 4 * D_MODEL

MLA_HEADS = 8
MLA_Q_RANK = 512
MLA_KV_RANK = 256
MLA_NOPE = 128
MLA_ROPE = 64
MLA_V = 128

NA_HEADS = 16
NA_HEAD_DIM = D_MODEL // NA_HEADS
NA_WIN_ROWS = 8
NA_WIN_COLS = 16

GQA_HEAD_DIM = 128
GQA_Q_HEADS = D_MODEL // GQA_HEAD_DIM
GQA_KV_HEADS = GQA_Q_HEADS // 4
GQA_GROUP = GQA_Q_HEADS // GQA_KV_HEADS

kernel_name = "hybrid_mla_na_gqa_dit_trunk"


def _layer_norm(x, g, b):
    xf = x.astype(jnp.float32)
    mu = jnp.mean(xf, -1, keepdims=True)
    var = jnp.mean(jnp.square(xf - mu), -1, keepdims=True)
    return ((xf - mu) * lax.rsqrt(var + LN_EPS) * g + b).astype(x.dtype)


def _rms_norm(x, g):
    xf = x.astype(jnp.float32)
    return (xf * lax.rsqrt(jnp.mean(xf * xf, -1, keepdims=True) + RMS_EPS) * g).astype(x.dtype)


def _rope_1d(x, pos):
    half = x.shape[-1] // 2
    freqs = ROPE_THETA ** (-jnp.arange(half, dtype=jnp.float32) / half)
    ang = pos.astype(jnp.float32)[:, None, None] * freqs
    cos, sin = jnp.cos(ang), jnp.sin(ang)
    xf = x.astype(jnp.float32)
    x1, x2 = xf[..., :half], xf[..., half:]
    return jnp.concatenate([x1 * cos - x2 * sin, x1 * sin + x2 * cos], -1).astype(x.dtype)


def _rope_2d(x, rows, cols):
    h = x.shape[-1] // 2
    return jnp.concatenate([_rope_1d(x[..., :h], rows), _rope_1d(x[..., h:], cols)], -1)


def _grid_positions(s):
    t = jnp.arange(s, dtype=jnp.int32)
    return t // GRID_W, t % GRID_W


def _block_attention(q, k, v, scale):
    b, sq, hk, g, dk = q.shape
    nb = sq // Q_BLOCK
    qb = q.reshape(b, nb, Q_BLOCK, hk, g, dk).transpose(1, 0, 2, 3, 4, 5)

    def step(qi):
        s = jnp.einsum('bqhgd,bkhd->bhgqk', qi, k).astype(jnp.float32) * scale
        p = jax.nn.softmax(s, axis=-1).astype(v.dtype)
        return jnp.einsum('bhgqk,bkhd->bqhgd', p, v)

    o = lax.map(step, qb)
    return o.transpose(1, 0, 2, 3, 4, 5).reshape(b, sq, hk * g * v.shape[-1])


def _mla_q(h, w_dq, q_norm, w_uq, rows, cols):
    b, s, _ = h.shape
    q = (_rms_norm(h @ w_dq, q_norm) @ w_uq).reshape(b, s, MLA_HEADS, MLA_NOPE + MLA_ROPE)
    q_nope, q_pe = q[..., :MLA_NOPE], q[..., MLA_NOPE:]
    if rows is not None:
        q_pe = _rope_2d(q_pe, rows, cols)
    return jnp.concatenate([q_nope, q_pe], -1)


def _mla_kv(h, w_dkv, kv_norm, w_ukv, rows, cols):
    b, s, _ = h.shape
    ckv = h @ w_dkv
    c_kv, k_pe = ckv[..., :MLA_KV_RANK], ckv[..., MLA_KV_RANK:]
    kv = (_rms_norm(c_kv, kv_norm) @ w_ukv).reshape(b, s, MLA_HEADS, MLA_NOPE + MLA_V)
    k_nope, v = kv[..., :MLA_NOPE], kv[..., MLA_NOPE:]
    k_pe = k_pe[:, :, None, :]
    if rows is not None:
        k_pe = _rope_2d(k_pe, rows, cols)
    k = jnp.concatenate([k_nope, jnp.broadcast_to(k_pe, (b, s, MLA_HEADS, MLA_ROPE))], -1)
    return k, v


def _mla_mixer(h, hc, w_dq, q_norm, w_uq, w_dkv, kv_norm, w_ukv, w_o, need_ctx):
    rows, cols = _grid_positions(h.shape[1])
    scale = (MLA_NOPE + MLA_ROPE) ** -0.5
    q = _mla_q(h, w_dq, q_norm, w_uq, rows, cols)
    k, v = _mla_kv(h, w_dkv, kv_norm, w_ukv, rows, cols)
    kc, vc = _mla_kv(hc, w_dkv, kv_norm, w_ukv, None, None)
    k_all = jnp.concatenate([kc, k], 1)
    v_all = jnp.concatenate([vc, v], 1)
    out = _block_attention(q[:, :, :, None], k_all, v_all, scale) @ w_o
    out_c = None
    if need_ctx:
        qc = _mla_q(hc, w_dq, q_norm, w_uq, None, None)
        out_c = _block_attention(qc[:, :, :, None], kc, vc, scale) @ w_o
    return out, out_c


def _na_mixer(h, hc, w_qkv, b_qkv, rpb, w_o, need_ctx):
    b, s, _ = h.shape
    n_rows = s // GRID_W
    kr = min(NA_WIN_ROWS, n_rows)
    scale = NA_HEAD_DIM ** -0.5

    def proj(t):
        qkv = (t @ w_qkv + b_qkv).reshape(t.shape[0], t.shape[1], 3, NA_HEADS, NA_HEAD_DIM)
        return qkv[:, :, 0], qkv[:, :, 1], qkv[:, :, 2]

    q, k, v = proj(h)
    qc, kc, vc = proj(hc)
    grid = (b, n_rows, GRID_W, NA_HEADS, NA_HEAD_DIM)
    q_grid, k_grid, v_grid = q.reshape(grid), k.reshape(grid), v.reshape(grid)

    r_idx = jnp.arange(n_rows, dtype=jnp.int32)
    row_start = jnp.clip(r_idx - kr // 2, 0, n_rows - kr)
    c_idx = jnp.arange(GRID_W, dtype=jnp.int32)
    col_start = jnp.clip(c_idx - NA_WIN_COLS // 2, 0, GRID_W - NA_WIN_COLS)
    col_valid = (c_idx[None, :] >= col_start[:, None]) & (c_idx[None, :] < col_start[:, None] + NA_WIN_COLS)
    mask = jnp.tile(col_valid, (1, kr))
    dc = jnp.clip(c_idx[None, :] - c_idx[:, None] + NA_WIN_COLS - 1, 0, 2 * NA_WIN_COLS - 2)
    rpb_cols = rpb[:, :, dc]

    def row_step(args):
        r, rs = args
        q_row = lax.dynamic_index_in_dim(q_grid, r, axis=1, keepdims=False)
        k_strip = lax.dynamic_slice_in_dim(k_grid, rs, kr, axis=1).reshape(b, kr * GRID_W, NA_HEADS, NA_HEAD_DIM)
        v_strip = lax.dynamic_slice_in_dim(v_grid, rs, kr, axis=1).reshape(b, kr * GRID_W, NA_HEADS, NA_HEAD_DIM)
        dr = rs + jnp.arange(kr, dtype=jnp.int32) - r + NA_WIN_ROWS - 1
        bias = jnp.take(rpb_cols, dr, axis=1).transpose(0, 2, 1, 3).reshape(NA_HEADS, GRID_W, kr * GRID_W)
        s_lat = jnp.einsum('bqhd,bkhd->bhqk', q_row, k_strip).astype(jnp.float32) * scale + bias
        s_lat = jnp.where(mask, s_lat, -jnp.inf)
        s_ctx = jnp.einsum('bqhd,bkhd->bhqk', q_row, kc).astype(jnp.float32) * scale
        p = jax.nn.softmax(jnp.concatenate([s_ctx, s_lat], -1), axis=-1).astype(v.dtype)
        return jnp.einsum('bhqk,bkhd->bqhd', p, jnp.concatenate([vc, v_strip], 1))

    o = lax.map(row_step, (r_idx, row_start))
    out = o.transpose(1, 0, 2, 3, 4).reshape(b, s, NA_HEADS * NA_HEAD_DIM) @ w_o
    out_c = None
    if need_ctx:
        out_c = _block_attention(qc[:, :, :, None], kc, vc, scale) @ w_o
    return out, out_c


def _gqa_proj(t, w_qkv, q_norm, k_norm, rows, cols):
    bb, ss, _ = t.shape
    nq = GQA_Q_HEADS * GQA_HEAD_DIM
    nk = GQA_KV_HEADS * GQA_HEAD_DIM
    qkv = t @ w_qkv
    q = _rms_norm(qkv[..., :nq].reshape(bb, ss, GQA_Q_HEADS, GQA_HEAD_DIM), q_norm)
    k = _rms_norm(qkv[..., nq:nq + nk].reshape(bb, ss, GQA_KV_HEADS, GQA_HEAD_DIM), k_norm)
    v = qkv[..., nq + nk:].reshape(bb, ss, GQA_KV_HEADS, GQA_HEAD_DIM)
    if rows is not None:
        q = _rope_2d(q, rows, cols)
        k = _rope_2d(k, rows, cols)
    return q.reshape(bb, ss, GQA_KV_HEADS, GQA_GROUP, GQA_HEAD_DIM), k, v


def _gqa_mixer(h, hc, w_qkv, q_norm, k_norm, w_o, need_ctx):
    rows, cols = _grid_positions(h.shape[1])
    scale = GQA_HEAD_DIM ** -0.5
    q, k, v = _gqa_proj(h, w_qkv, q_norm, k_norm, rows, cols)
    qc, kc, vc = _gqa_proj(hc, w_qkv, q_norm, k_norm, None, None)
    out = _block_attention(q, jnp.concatenate([kc, k], 1), jnp.concatenate([vc, v], 1), scale) @ w_o
    out_c = None
    if need_ctx:
        out_c = _block_attention(qc, kc, vc, scale) @ w_o
    return out, out_c


def _mlp(h, w1, w2):
    return jnp.square(jax.nn.relu(h @ w1)) @ w2


def setup_inputs(seed: int = 0) -> dict:
    key = jax.random.key(seed)
    ks = iter(jax.random.split(key, 32))
    beta = (8.0 * DEPTH) ** -0.25
    n_a = len(range(0, DEPTH, N_MIXERS))
    n_b = len(range(1, DEPTH, N_MIXERS))
    n_c = len(range(2, DEPTH, N_MIXERS))
    D = D_MODEL

    def nrm(shape, s=1.0):
        return jax.random.normal(next(ks), shape, jnp.float32) * s

    def w(shape, fan_in, g=1.0):
        return nrm(shape, g * fan_in ** -0.5)

    def gain(shape):
        return 1.0 + nrm(shape, 0.05)

    return {
        "x": nrm((BATCH, SEQ, D)),
        "c": nrm((BATCH, D)),
        "ctx": nrm((BATCH, CTX_LEN, D)),
        "c_ctx": nrm((D,)),
        "ada_w": w((DEPTH, D, 6 * D), D),
        "ada_b": nrm((DEPTH, 6 * D), 0.02),
        "ln1_g": gain((DEPTH, D)),
        "ln1_b": nrm((DEPTH, D), 0.02),
        "ln2_g": gain((DEPTH, D)),
        "ln2_b": nrm((DEPTH, D), 0.02),
        "mlp_w1": w((DEPTH, D, D_FF), D),
        "mlp_w2": w((DEPTH, D_FF, D), D_FF, beta),
        "mla_w_dq": w((n_a, D, MLA_Q_RANK), D),
        "mla_q_norm": gain((n_a, MLA_Q_RANK)),
        "mla_w_uq": w((n_a, MLA_Q_RANK, MLA_HEADS * (MLA_NOPE + MLA_ROPE)), MLA_Q_RANK),
        "mla_w_dkv": w((n_a, D, MLA_KV_RANK + MLA_ROPE), D),
        "mla_kv_norm": gain((n_a, MLA_KV_RANK)),
        "mla_w_ukv": w((n_a, MLA_KV_RANK, MLA_HEADS * (MLA_NOPE + MLA_V)), MLA_KV_RANK),
        "mla_w_o": w((n_a, MLA_HEADS * MLA_V, D), MLA_HEADS * MLA_V, beta),
        "na_w_qkv": w((n_b, D, 3 * NA_HEADS * NA_HEAD_DIM), D),
        "na_b_qkv": nrm((n_b, 3 * NA_HEADS * NA_HEAD_DIM), 0.02),
        "na_rpb": nrm((n_b, NA_HEADS, 2 * NA_WIN_ROWS - 1, 2 * NA_WIN_COLS - 1), 0.1),
        "na_w_o": w((n_b, NA_HEADS * NA_HEAD_DIM, D), NA_HEADS * NA_HEAD_DIM, beta),
        "gqa_w_qkv": w((n_c, D, (GQA_Q_HEADS + 2 * GQA_KV_HEADS) * GQA_HEAD_DIM), D),
        "gqa_q_norm": gain((n_c, GQA_HEAD_DIM)),
        "gqa_k_norm": gain((n_c, GQA_HEAD_DIM)),
        "gqa_w_o": w((n_c, GQA_Q_HEADS * GQA_HEAD_DIM, D), GQA_Q_HEADS * GQA_HEAD_DIM, beta),
    }


def reference(x, c, ctx, c_ctx, ada_w, ada_b, ln1_g, ln1_b, ln2_g, ln2_b, mlp_w1, mlp_w2,
              mla_w_dq, mla_q_norm, mla_w_uq, mla_w_dkv, mla_kv_norm, mla_w_ukv, mla_w_o,
              na_w_qkv, na_b_qkv, na_rpb, na_w_o,
              gqa_w_qkv, gqa_q_norm, gqa_k_norm, gqa_w_o):
    alpha = (2.0 * DEPTH) ** 0.25
    for i in range(DEPTH):
        kind, j = i % N_MIXERS, i // N_MIXERS
        need_ctx = i < DEPTH - 1
        mod = (jax.nn.silu(c) @ ada_w[i] + ada_b[i])[:, None, :]
        mod_c = jax.nn.silu(c_ctx) @ ada_w[i] + ada_b[i]
        sh1, sc1, g1, sh2, sc2, g2 = jnp.split(mod, 6, axis=-1)
        csh1, csc1, cg1, csh2, csc2, cg2 = jnp.split(mod_c, 6, axis=-1)

        h = x * (1 + sc1) + sh1
        hc = ctx * (1 + csc1) + csh1
        if kind == 0:
            y, yc = _mla_mixer(h, hc, mla_w_dq[j], mla_q_norm[j], mla_w_uq[j], mla_w_dkv[j],
                               mla_kv_norm[j], mla_w_ukv[j], mla_w_o[j], need_ctx)
        elif kind == 1:
            y, yc = _na_mixer(h, hc, na_w_qkv[j], na_b_qkv[j], na_rpb[j], na_w_o[j], need_ctx)
        else:
            y, yc = _gqa_mixer(h, hc, gqa_w_qkv[j], gqa_q_norm[j], gqa_k_norm[j], gqa_w_o[j], need_ctx)
        x = _layer_norm(alpha * x + g1 * y, ln1_g[i], ln1_b[i])
        if need_ctx:
            ctx = _layer_norm(alpha * ctx + cg1 * yc, ln1_g[i], ln1_b[i])

        x = _layer_norm(alpha * x + g2 * _mlp(x * (1 + sc2) + sh2, mlp_w1[i], mlp_w2[i]), ln2_g[i], ln2_b[i])
        if need_ctx:
            ctx = _layer_norm(alpha * ctx + cg2 * _mlp(ctx * (1 + csc2) + csh2, mlp_w1[i], mlp_w2[i]),
                              ln2_g[i], ln2_b[i])
    return x
```

```python
import functools

import jax
import jax.numpy as jnp
from jax import lax
from jax.experimental import pallas as pl
from jax.experimental.pallas import tpu as pltpu

D_MODEL = 1024
BATCH = 16
SEQ = 2048
DEPTH = 4
GRID_W = 64
CTX_LEN = 256
N_MIXERS = 3
ROPE_THETA = 10000.0
LN_EPS = 1e-5
RMS_EPS = 1e-6
D_FF = 4 * D_MODEL

MLA_HEADS = 8
MLA_Q_RANK = 512
MLA_KV_RANK = 256
MLA_NOPE = 128
MLA_ROPE = 64
MLA_V = 128
MLA_QK_PAD = 256

NA_HEADS = 16
NA_HEAD_DIM = D_MODEL // NA_HEADS
NA_WIN_ROWS = 8
NA_WIN_COLS = 16
NA_Q_ROWS = 4
NA_K_ROWS = 12
N_ROWS = SEQ // GRID_W

GQA_HEAD_DIM = 128
GQA_Q_HEADS = D_MODEL // GQA_HEAD_DIM
GQA_KV_HEADS = GQA_Q_HEADS // 4
GQA_GROUP = GQA_Q_HEADS // GQA_KV_HEADS

ALPHA = (2.0 * DEPTH) ** 0.25
LANES = 128
ROW_TILE = 512
F32 = jnp.float32
BF16 = jnp.bfloat16
MIB = 1024 * 1024


def _params(vmem_mib, n_axes):
    return pltpu.CompilerParams(
        dimension_semantics=("arbitrary",) * n_axes, vmem_limit_bytes=vmem_mib * MIB
    )


def _full_spec(arr, n_axes, single=False):
    zeros = (0,) * arr.ndim
    index_map = {2: lambda a, b: zeros, 3: lambda a, b, c: zeros}[n_axes]
    if single:
        return pl.BlockSpec(arr.shape, index_map, pipeline_mode=pl.Buffered(1))
    return pl.BlockSpec(arr.shape, index_map)


def _dot(a, b):
    return jnp.dot(a, b, preferred_element_type=F32)


def _dot_nt(a, b):
    return lax.dot_general(a, b, (((1,), (1,)), ((), ())), preferred_element_type=F32)


def _modulate(x, m_ref, shift_row, scale_row):
    shift = m_ref[0, shift_row:shift_row + 1, :]
    scale = m_ref[0, scale_row:scale_row + 1, :]
    return x * (1.0 + scale) + shift


def _rms(t, g):
    return t * lax.rsqrt(jnp.mean(t * t, -1, keepdims=True) + RMS_EPS) * g


def _layer_norm(z, g, b):
    mu = jnp.mean(z, -1, keepdims=True)
    zc = z - mu
    var = jnp.mean(zc * zc, -1, keepdims=True)
    return zc * lax.rsqrt(var + LN_EPS) * g + b


def _rot_half(t, half):
    n = t.shape[-1]
    lane = lax.broadcasted_iota(jnp.int32, t.shape, 1)
    low = (lane & (2 * half - 1)) < half
    return jnp.where(low, pltpu.roll(t, n - half, 1), pltpu.roll(t, half, 1))


def _rope_tables(d):
    t = jnp.arange(SEQ, dtype=jnp.int32)
    rows, cols = t // GRID_W, t % GRID_W
    half = d // 4
    freqs = ROPE_THETA ** (-jnp.arange(half, dtype=F32) / half)

    def one(pos):
        ang = pos.astype(F32)[:, None] * freqs
        c, s = jnp.cos(ang), jnp.sin(ang)
        return jnp.concatenate([c, c], -1), jnp.concatenate([-s, s], -1)

    cr, sr = one(rows)
    cc, sc = one(cols)
    cos = jnp.concatenate([cr, cc], -1)
    sin = jnp.concatenate([sr, sc], -1)
    pad = ((0, 0), (0, LANES - d))
    return jnp.pad(cos, pad), jnp.pad(sin, pad)


def _ada_body(c_ref, w_ref, b_ref, o_ref):
    c = c_ref[...]
    s = (c * (1.0 / (1.0 + jnp.exp(-c)))).astype(BF16)
    o_ref[0] = _dot(s, w_ref[0].astype(BF16)) + b_ref[0]


def _ada_all(cs, ada_w, ada_b):
    rows = cs.shape[0]
    tn = 1536
    return pl.pallas_call(
        _ada_body,
        out_shape=jax.ShapeDtypeStruct((DEPTH, rows, 6 * D_MODEL), F32),
        grid=(DEPTH, 6 * D_MODEL // tn),
        in_specs=[
            pl.BlockSpec((rows, D_MODEL), lambda i, n: (0, 0)),
            pl.BlockSpec((1, D_MODEL, tn), lambda i, n: (i, 0, n)),
            pl.BlockSpec((1, 1, tn), lambda i, n: (i, 0, n)),
        ],
        out_specs=pl.BlockSpec((1, rows, tn), lambda i, n: (i, 0, n)),
        compiler_params=_params(40, 2),
        name="ada_mod",
    )(cs, ada_w, ada_b.reshape(DEPTH, 1, 6 * D_MODEL))


def _tok_spec(width, tm=ROW_TILE):
    return pl.BlockSpec((1, tm, width), lambda b, t: (b, t, 0))


def _mod_spec():
    return pl.BlockSpec((1, 6, D_MODEL), lambda b, t: (b, 0, 0))


def _rope_spec(tm=ROW_TILE):
    return pl.BlockSpec((tm, LANES), lambda b, t: (t, 0))


def _gqa_proj_body(*refs, rope):
    if rope:
        x_ref, m_ref, w_ref, qn_ref, kn_ref, cos_ref, sin_ref, q_ref, k_ref, v_ref = refs
        cos, sin = cos_ref[...], sin_ref[...]
    else:
        x_ref, m_ref, w_ref, qn_ref, kn_ref, q_ref, k_ref, v_ref = refs
    h = _modulate(x_ref[0], m_ref, 0, 1).astype(BF16)
    qkv = _dot(h, w_ref[...])
    scale = GQA_HEAD_DIM ** -0.5
    hd = GQA_HEAD_DIM
    for i in range(GQA_Q_HEADS + GQA_KV_HEADS):
        is_q = i < GQA_Q_HEADS
        t = _rms(qkv[:, i * hd:(i + 1) * hd], qn_ref[...] if is_q else kn_ref[...])
        if rope:
            t = t * cos + _rot_half(t, hd // 4) * sin
        if is_q:
            q_ref[0, :, i * hd:(i + 1) * hd] = (t * scale).astype(BF16)
        else:
            j = i - GQA_Q_HEADS
            k_ref[0, :, j * hd:(j + 1) * hd] = t.astype(BF16)
    v_ref[0] = qkv[:, (GQA_Q_HEADS + GQA_KV_HEADS) * hd:].astype(BF16)


def _gqa_proj(x, mod, w, qn, kn, tables):
    bx, sx, _ = x.shape
    rope = tables is not None
    nk = GQA_KV_HEADS * GQA_HEAD_DIM
    ins = [x, mod, w, qn, kn]
    specs = [_tok_spec(D_MODEL), _mod_spec(), _full_spec(w, 2), _full_spec(qn, 2), _full_spec(kn, 2)]
    if rope:
        ins += list(tables)
        specs += [_rope_spec(), _rope_spec()]
    return pl.pallas_call(
        functools.partial(_gqa_proj_body, rope=rope),
        out_shape=(
            jax.ShapeDtypeStruct((bx, sx, D_MODEL), BF16),
            jax.ShapeDtypeStruct((bx, sx, nk), BF16),
            jax.ShapeDtypeStruct((bx, sx, nk), BF16),
        ),
        grid=(bx, sx // ROW_TILE),
        in_specs=specs,
        out_specs=(_tok_spec(D_MODEL), _tok_spec(nk), _tok_spec(nk)),
        compiler_params=_params(48, 2),
        name="gqa_proj",
    )(*ins)


def _mla_proj_body(*refs, rope, need_q):
    refs = list(refs)
    x_ref, m_ref = refs[:2]
    del refs[:2]
    if need_q:
        wdq_ref, qn_ref, wuq_ref = refs[:3]
        del refs[:3]
    wdkv_ref, kvn_ref, wukv_ref = refs[:3]
    del refs[:3]
    if rope:
        cos, sin = refs[0][...], refs[1][...]
        del refs[:2]
    if need_q:
        q_ref = refs.pop(0)
    k_ref, v_ref = refs

    h = _modulate(x_ref[0], m_ref, 0, 1).astype(BF16)
    scale = (MLA_NOPE + MLA_ROPE) ** -0.5
    pw = MLA_QK_PAD

    def rotary(pe):
        return pe * cos + _rot_half(pe, MLA_ROPE // 4) * sin if rope else pe

    if need_q:
        cq = _rms(_dot(h, wdq_ref[...]), qn_ref[...]).astype(BF16)
        q = _dot(cq, wuq_ref[...])
        for i in range(MLA_HEADS):
            nope = q[:, i * pw:i * pw + MLA_NOPE]
            pe = rotary(q[:, i * pw + MLA_NOPE:(i + 1) * pw])
            q_ref[0, :, i * pw:i * pw + MLA_NOPE] = (nope * scale).astype(BF16)
            q_ref[0, :, i * pw + MLA_NOPE:(i + 1) * pw] = (pe * scale).astype(BF16)

    ckv = _dot(h, wdkv_ref[...])
    c_kv = _rms(ckv[:, :MLA_KV_RANK], kvn_ref[...]).astype(BF16)
    k_pe = rotary(ckv[:, MLA_KV_RANK:]).astype(BF16)
    kv = _dot(c_kv, wukv_ref[...])
    for i in range(MLA_HEADS):
        k_ref[0, :, i * pw:i * pw + MLA_NOPE] = kv[:, i * MLA_NOPE:(i + 1) * MLA_NOPE].astype(BF16)
        k_ref[0, :, i * pw + MLA_NOPE:(i + 1) * pw] = k_pe
    v_ref[0] = kv[:, MLA_HEADS * MLA_NOPE:].astype(BF16)


def _mla_proj(x, mod, wq, wkv, tables, need_q):
    bx, sx, _ = x.shape
    rope = tables is not None
    ins = [x, mod]
    specs = [_tok_spec(D_MODEL), _mod_spec()]
    weights = (list(wq) if need_q else []) + list(wkv)
    ins += weights
    specs += [_full_spec(w, 2) for w in weights]
    if rope:
        ins += list(tables)
        specs += [_rope_spec(), _rope_spec()]
    kw = MLA_HEADS * MLA_QK_PAD
    vw = MLA_HEADS * MLA_V
    shapes = [jax.ShapeDtypeStruct((bx, sx, kw), BF16), jax.ShapeDtypeStruct((bx, sx, vw), BF16)]
    ospecs = [_tok_spec(kw), _tok_spec(vw)]
    if need_q:
        shapes.insert(0, jax.ShapeDtypeStruct((bx, sx, kw), BF16))
        ospecs.insert(0, _tok_spec(kw))
    out = pl.pallas_call(
        functools.partial(_mla_proj_body, rope=rope, need_q=need_q),
        out_shape=tuple(shapes),
        grid=(bx, sx // ROW_TILE),
        in_specs=specs,
        out_specs=tuple(ospecs),
        compiler_params=_params(48, 2),
        name="mla_proj",
    )(*ins)
    return out if need_q else (None,) + tuple(out)


def _na_proj_body(x_ref, m_ref, w_ref, b_ref, q_ref, k_ref, v_ref):
    h = _modulate(x_ref[0], m_ref, 0, 1).astype(BF16)
    qkv = _dot(h, w_ref[...]) + b_ref[...]
    scale = NA_HEAD_DIM ** -0.5
    q_ref[0] = (qkv[:, :D_MODEL] * scale).astype(BF16)
    k_ref[0] = qkv[:, D_MODEL:2 * D_MODEL].astype(BF16)
    v_ref[0] = qkv[:, 2 * D_MODEL:].astype(BF16)


def _na_proj(x, mod, w, b):
    bx, sx, _ = x.shape
    shape = jax.ShapeDtypeStruct((bx, sx, D_MODEL), BF16)
    return pl.pallas_call(
        _na_proj_body,
        out_shape=(shape, shape, shape),
        grid=(bx, sx // ROW_TILE),
        in_specs=[_tok_spec(D_MODEL), _mod_spec(), _full_spec(w, 2), _full_spec(b, 2)],
        out_specs=(_tok_spec(D_MODEL),) * 3,
        compiler_params=_params(48, 2),
        name="na_proj",
    )(x, mod, w, b)


def _softmax_pv(q, keys, values, biases):
    scores = [_dot_nt(q, k) if b is None else _dot_nt(q, k) + b for k, b in zip(keys, biases)]
    m = functools.reduce(jnp.maximum, [jnp.max(s, -1, keepdims=True) for s in scores])
    probs = [jnp.exp(s - m) for s in scores]
    denom = functools.reduce(jnp.add, [jnp.sum(p, -1, keepdims=True) for p in probs])
    out = functools.reduce(jnp.add, [_dot(p.astype(BF16), v) for p, v in zip(probs, values)])
    return out * (1.0 / denom)


def _attn_body(*refs, heads, group, dk, dv, has_lat):
    if has_lat:
        q_ref, kc_ref, vc_ref, k_ref, v_ref, o_ref = refs
    else:
        q_ref, kc_ref, vc_ref, o_ref = refs
    outs = []
    for i in range(heads):
        keys = [kc_ref[0, :, i * dk:(i + 1) * dk]]
        values = [vc_ref[0, :, i * dv:(i + 1) * dv]]
        if has_lat:
            keys.append(k_ref[0, :, i * dk:(i + 1) * dk])
            values.append(v_ref[0, :, i * dv:(i + 1) * dv])
        for g in range(group):
            c0 = (i * group + g) * dk
            outs.append(_softmax_pv(q_ref[0, :, c0:c0 + dk], keys, values, [None] * len(keys)))
    o_ref[0] = jnp.concatenate(outs, -1).astype(o_ref.dtype)


def _attention(q, kc, vc, k, v, *, heads, group, dk, dv, tq):
    b, sq, qw = q.shape
    has_lat = k is not None
    n_blocks = qw // (heads * group * dk)
    qb, kb, vb = heads * group * dk, heads * dk, heads * dv
    ins = [q, kc, vc]
    specs = [
        pl.BlockSpec((1, tq, qb), lambda bi, hi, ti: (bi, ti, hi)),
        pl.BlockSpec((1, kc.shape[1], kb), lambda bi, hi, ti: (bi, 0, hi)),
        pl.BlockSpec((1, vc.shape[1], vb), lambda bi, hi, ti: (bi, 0, hi)),
    ]
    if has_lat:
        ins += [k, v]
        specs += [
            pl.BlockSpec((1, k.shape[1], kb), lambda bi, hi, ti: (bi, 0, hi)),
            pl.BlockSpec((1, v.shape[1], vb), lambda bi, hi, ti: (bi, 0, hi)),
        ]
    ob = heads * group * dv
    return pl.pallas_call(
        functools.partial(_attn_body, heads=heads, group=group, dk=dk, dv=dv, has_lat=has_lat),
        out_shape=jax.ShapeDtypeStruct((b, sq, n_blocks * ob), BF16),
        grid=(b, n_blocks, sq // tq),
        in_specs=specs,
        out_specs=pl.BlockSpec((1, tq, ob), lambda bi, hi, ti: (bi, ti, hi)),
        compiler_params=_params(48, 3),
        name="attn_lat" if has_lat else "attn_ctx",
    )(*ins)


def _na_window(j):
    n_blocks = N_ROWS // NA_Q_ROWS
    w0 = min(max(NA_Q_ROWS * j - NA_WIN_ROWS // 2, 0), N_ROWS - NA_K_ROWS)
    table = 0 if j == 0 else (2 if j == n_blocks - 1 else 1)
    return w0, table


def _na_bias_tables(rpb):
    n_blocks = N_ROWS // NA_Q_ROWS
    c = jnp.arange(GRID_W, dtype=jnp.int32)
    col_start = jnp.clip(c - NA_WIN_COLS // 2, 0, GRID_W - NA_WIN_COLS)
    col_ok = (c[None, :] >= col_start[:, None]) & (c[None, :] < col_start[:, None] + NA_WIN_COLS)
    dc = jnp.clip(c[None, :] - c[:, None] + NA_WIN_COLS - 1, 0, 2 * NA_WIN_COLS - 2)
    rpb_cols = rpb[:, :, dc]
    tables = []
    for j in (0, 1, n_blocks - 1):
        w0, _ = _na_window(j)
        r = NA_Q_ROWS * j + jnp.arange(NA_Q_ROWS, dtype=jnp.int32)
        kr = w0 + jnp.arange(NA_K_ROWS, dtype=jnp.int32)
        rs = jnp.clip(r - NA_WIN_ROWS // 2, 0, N_ROWS - NA_WIN_ROWS)
        row_ok = (kr[None, :] >= rs[:, None]) & (kr[None, :] < rs[:, None] + NA_WIN_ROWS)
        dr = jnp.clip(kr[None, :] - r[:, None] + NA_WIN_ROWS - 1, 0, 2 * NA_WIN_ROWS - 2)
        bias = jnp.take(rpb_cols, dr.reshape(-1), axis=1)
        bias = bias.reshape(NA_HEADS, NA_Q_ROWS, NA_K_ROWS, GRID_W, GRID_W)
        ok = row_ok[:, :, None, None] & col_ok[None, None, :, :]
        bias = jnp.where(ok[None], bias, -jnp.inf)
        bias = bias.transpose(0, 1, 3, 2, 4)
        tables.append(bias.reshape(NA_HEADS, NA_Q_ROWS * GRID_W, NA_K_ROWS * GRID_W))
    return jnp.stack(tables)


def _na_attn_body(q_ref, kc_ref, vc_ref, k_ref, v_ref, bias_ref, o_ref):
    hd = NA_HEAD_DIM
    heads = LANES // hd
    qn = NA_Q_ROWS * GRID_W
    for j in range(N_ROWS // NA_Q_ROWS):
        w0, table = _na_window(j)
        krows = slice(w0 * GRID_W, (w0 + NA_K_ROWS) * GRID_W)
        outs = []
        for i in range(heads):
            cols = slice(i * hd, (i + 1) * hd)
            outs.append(_softmax_pv(
                q_ref[0, j * qn:(j + 1) * qn, cols],
                [kc_ref[0, :, cols], k_ref[0, krows, cols]],
                [vc_ref[0, :, cols], v_ref[0, krows, cols]],
                [None, bias_ref[table, i]],
            ))
        o_ref[0, j * qn:(j + 1) * qn, :] = jnp.concatenate(outs, -1).astype(o_ref.dtype)


def _na_attention(q, kc, vc, k, v, bias):
    b, s, _ = q.shape
    n_pairs = D_MODEL // LANES
    heads = LANES // NA_HEAD_DIM
    lat = pl.BlockSpec((1, s, LANES), lambda hi, bi: (bi, 0, hi))
    ctx = pl.BlockSpec((1, CTX_LEN, LANES), lambda hi, bi: (bi, 0, hi))
    bias_spec = pl.BlockSpec((3, heads) + bias.shape[2:], lambda hi, bi: (0, hi, 0, 0))
    return pl.pallas_call(
        _na_attn_body,
        out_shape=jax.ShapeDtypeStruct((b, s, D_MODEL), BF16),
        grid=(n_pairs, b),
        in_specs=[lat, ctx, ctx, lat, lat, bias_spec],
        out_specs=lat,
        compiler_params=_params(48, 2),
        name="na_attn",
    )(q, kc, vc, k, v, bias)


def _oproj_ln_body(a_ref, w_ref, x_ref, m_ref, g_ref, b_ref, o_ref):
    y = _dot(a_ref[0], w_ref[...])
    z = ALPHA * x_ref[0] + m_ref[0, 2:3, :] * y
    o_ref[0] = _layer_norm(z, g_ref[...], b_ref[...])


def _oproj_ln(a, w, x, mod, g, b):
    bx, sx, _ = x.shape
    return pl.pallas_call(
        _oproj_ln_body,
        out_shape=jax.ShapeDtypeStruct(x.shape, F32),
        grid=(bx, sx // ROW_TILE),
        in_specs=[_tok_spec(a.shape[-1]), _full_spec(w, 2), _tok_spec(D_MODEL), _mod_spec(),
                  _full_spec(g, 2), _full_spec(b, 2)],
        out_specs=_tok_spec(D_MODEL),
        compiler_params=_params(48, 2),
        name="oproj_ln",
    )(a, w, x, mod, g, b)


MLP_CHUNK = 1024


def _mlp_ln_body(x_ref, m_ref, w1_ref, w2_ref, g_ref, b_ref, o_ref):
    x = x_ref[0]
    h = _modulate(x, m_ref, 3, 4).astype(BF16)
    acc = None
    for c in range(D_FF // MLP_CHUNK):
        cols = slice(c * MLP_CHUNK, (c + 1) * MLP_CHUNK)
        u = jnp.square(jnp.maximum(_dot(h, w1_ref[:, cols]), 0.0)).astype(BF16)
        part = _dot(u, w2_ref[cols, :])
        acc = part if acc is None else acc + part
    z = ALPHA * x + m_ref[0, 5:6, :] * acc
    o_ref[0] = _layer_norm(z, g_ref[...], b_ref[...])


def _mlp_ln(x, mod, w1, w2, g, b):
    bx, sx, _ = x.shape
    return pl.pallas_call(
        _mlp_ln_body,
        out_shape=jax.ShapeDtypeStruct(x.shape, F32),
        grid=(bx, sx // ROW_TILE),
        in_specs=[_tok_spec(D_MODEL), _mod_spec(), _full_spec(w1, 2, single=True),
                  _full_spec(w2, 2, single=True), _full_spec(g, 2), _full_spec(b, 2)],
        out_specs=_tok_spec(D_MODEL),
        compiler_params=_params(56, 2),
        name="mlp_ln",
    )(x, mod, w1, w2, g, b)


def _row(v):
    return v.reshape(1, -1)


def kernel(x, c, ctx, c_ctx, ada_w, ada_b, ln1_g, ln1_b, ln2_g, ln2_b, mlp_w1, mlp_w2, mla_w_dq, mla_q_norm, mla_w_uq, mla_w_dkv, mla_kv_norm, mla_w_ukv, mla_w_o, na_w_qkv, na_b_qkv, na_rpb, na_w_o, gqa_w_qkv, gqa_q_norm, gqa_k_norm, gqa_w_o):
    pad_rows = 24 - BATCH - 1
    cs = jnp.concatenate([c, c_ctx[None, :], jnp.zeros((pad_rows, D_MODEL), F32)], 0)
    mod_all = _ada_all(cs, ada_w, ada_b).reshape(DEPTH, 24, 6, D_MODEL)

    xc = ctx.reshape(1, BATCH * CTX_LEN, D_MODEL)
    gqa_tables = _rope_tables(GQA_HEAD_DIM)
    mla_tables = _rope_tables(MLA_ROPE)

    def per_batch(t):
        return t.reshape(BATCH, CTX_LEN, t.shape[-1])

    for i in range(DEPTH):
        kind, j = i % N_MIXERS, i // N_MIXERS
        need_ctx = i < DEPTH - 1
        mod = mod_all[i, :BATCH]
        mod_c = mod_all[i, BATCH:BATCH + 1]

        if kind == 0:
            wuq = mla_w_uq[j].reshape(MLA_Q_RANK, MLA_HEADS, MLA_NOPE + MLA_ROPE)
            wuq = jnp.pad(wuq, ((0, 0), (0, 0), (0, MLA_QK_PAD - MLA_NOPE - MLA_ROPE)))
            wq = (mla_w_dq[j].astype(BF16), _row(mla_q_norm[j]),
                  wuq.reshape(MLA_Q_RANK, MLA_HEADS * MLA_QK_PAD).astype(BF16))
            wdkv = jnp.pad(mla_w_dkv[j], ((0, 0), (0, LANES - MLA_ROPE)))
            wukv = mla_w_ukv[j].reshape(MLA_KV_RANK, MLA_HEADS, 2, MLA_NOPE)
            wukv = wukv.transpose(0, 2, 1, 3).reshape(MLA_KV_RANK, 2 * MLA_HEADS * MLA_NOPE)
            wkv = (wdkv.astype(BF16), _row(mla_kv_norm[j]), wukv.astype(BF16))
            q, k, v = _mla_proj(x, mod, wq, wkv, mla_tables, True)
            qc, kc, vc = _mla_proj(xc, mod_c, wq, wkv, None, need_ctx)
            kc, vc = per_batch(kc), per_batch(vc)
            dims = dict(heads=1, group=1, dk=MLA_QK_PAD, dv=MLA_V)
            y = _attention(q, kc, vc, k, v, tq=512, **dims)
            if need_ctx:
                yc = _attention(per_batch(qc), kc, vc, None, None, tq=CTX_LEN, **dims)
            w_o = mla_w_o[j]
        elif kind == 1:
            w, b = na_w_qkv[j].astype(BF16), _row(na_b_qkv[j])
            q, k, v = _na_proj(x, mod, w, b)
            qc, kc, vc = _na_proj(xc, mod_c, w, b)
            qc, kc, vc = per_batch(qc), per_batch(kc), per_batch(vc)
            y = _na_attention(q, kc, vc, k, v, _na_bias_tables(na_rpb[j]))
            if need_ctx:
                yc = _attention(qc, kc, vc, None, None, tq=CTX_LEN, heads=LANES // NA_HEAD_DIM,
                                group=1, dk=NA_HEAD_DIM, dv=NA_HEAD_DIM)
            w_o = na_w_o[j]
        else:
            w = gqa_w_qkv[j].astype(BF16)
            qn, kn = _row(gqa_q_norm[j]), _row(gqa_k_norm[j])
            q, k, v = _gqa_proj(x, mod, w, qn, kn, gqa_tables)
            qc, kc, vc = _gqa_proj(xc, mod_c, w, qn, kn, None)
            qc, kc, vc = per_batch(qc), per_batch(kc), per_batch(vc)
            dims = dict(heads=1, group=GQA_GROUP, dk=GQA_HEAD_DIM, dv=GQA_HEAD_DIM)
            y = _attention(q, kc, vc, k, v, tq=256, **dims)
            if need_ctx:
                yc = _attention(qc, kc, vc, None, None, tq=CTX_LEN, **dims)
            w_o = gqa_w_o[j]

        w_o = w_o.astype(BF16)
        w1, w2 = mlp_w1[i].astype(BF16), mlp_w2[i].astype(BF16)
        g1, b1, g2, b2 = _row(ln1_g[i]), _row(ln1_b[i]), _row(ln2_g[i]), _row(ln2_b[i])
        x = _oproj_ln(y, w_o, x, mod, g1, b1)
        x = _mlp_ln(x, mod, w1, w2, g2, b2)
        if need_ctx:
            yc = yc.reshape(1, BATCH * CTX_LEN, D_MODEL)
            xc = _oproj_ln(yc, w_o, xc, mod_c, g1, b1)
            xc = _mlp_ln(xc, mod_c, w1, w2, g2, b2)
    return x
```

```python
import functools

import jax
import jax.numpy as jnp
from jax import lax
from jax.experimental import pallas as pl
from jax.experimental.pallas import tpu as pltpu

D_MODEL = 1024
BATCH = 16
SEQ = 2048
DEPTH = 4
GRID_W = 64
CTX_LEN = 256
N_MIXERS = 3
ROPE_THETA = 10000.0
LN_EPS = 1e-5
RMS_EPS = 1e-6
D_FF = 4 * D_MODEL

MLA_HEADS = 8
MLA_Q_RANK = 512
MLA_KV_RANK = 256
MLA_NOPE = 128
MLA_ROPE = 64
MLA_V = 128
MLA_QK_PAD = 256

NA_HEADS = 16
NA_HEAD_DIM = D_MODEL // NA_HEADS
NA_WIN_ROWS = 8
NA_WIN_COLS = 16
NA_Q_ROWS = 4
NA_K_ROWS = 12
N_ROWS = SEQ // GRID_W

GQA_HEAD_DIM = 128
GQA_Q_HEADS = D_MODEL // GQA_HEAD_DIM
GQA_KV_HEADS = GQA_Q_HEADS // 4
GQA_GROUP = GQA_Q_HEADS // GQA_KV_HEADS

ALPHA = (2.0 * DEPTH) ** 0.25
LANES = 128
ROW_TILE = 512
F32 = jnp.float32
BF16 = jnp.bfloat16
MIB = 1024 * 1024


def _params(vmem_mib, n_axes):
    return pltpu.CompilerParams(
        dimension_semantics=("arbitrary",) * n_axes, vmem_limit_bytes=vmem_mib * MIB
    )


def _full_spec(arr, n_axes, single=False):
    zeros = (0,) * arr.ndim
    index_map = {2: lambda a, b: zeros, 3: lambda a, b, c: zeros}[n_axes]
    if single:
        return pl.BlockSpec(arr.shape, index_map, pipeline_mode=pl.Buffered(1))
    return pl.BlockSpec(arr.shape, index_map)


def _dot(a, b):
    return jnp.dot(a, b, preferred_element_type=F32)


def _dot_nt(a, b):
    return lax.dot_general(a, b, (((1,), (1,)), ((), ())), preferred_element_type=F32)


def _modulate(x, m_ref, shift_row, scale_row):
    shift = m_ref[0, shift_row:shift_row + 1, :]
    scale = m_ref[0, scale_row:scale_row + 1, :]
    return x * (1.0 + scale) + shift


def _rms(t, g):
    return t * lax.rsqrt(jnp.mean(t * t, -1, keepdims=True) + RMS_EPS) * g


def _layer_norm(z, g, b):
    mu = jnp.mean(z, -1, keepdims=True)
    zc = z - mu
    var = jnp.mean(zc * zc, -1, keepdims=True)
    return zc * lax.rsqrt(var + LN_EPS) * g + b


HALF_LANES = LANES // 2


def _rot_half(t):
    return pltpu.roll(t, HALF_LANES, 1)


def _split_rotary(t, d):
    lead = t.shape[:-1]
    t = t.reshape(lead + (2, 2, d // 4)).swapaxes(-3, -2).reshape(lead + (2, d // 2))
    t = jnp.pad(t, [(0, 0)] * (len(lead) + 1) + [(0, HALF_LANES - d // 2)])
    return t.reshape(lead + (LANES,))


def _rope_tables(d):
    t = jnp.arange(SEQ, dtype=jnp.int32)
    rows, cols = t // GRID_W, t % GRID_W
    quarter = d // 4
    freqs = ROPE_THETA ** (-jnp.arange(quarter, dtype=F32) / quarter)
    ang = jnp.concatenate([rows.astype(F32)[:, None] * freqs, cols.astype(F32)[:, None] * freqs], -1)
    c, s = jnp.cos(ang), jnp.sin(ang)
    pad = jnp.zeros((SEQ, HALF_LANES - d // 2), F32)
    return jnp.concatenate([c, pad, c, pad], -1), jnp.concatenate([-s, pad, s, pad], -1)


def _ada_body(c_ref, w_ref, b_ref, o_ref):
    c = c_ref[...]
    s = (c * (1.0 / (1.0 + jnp.exp(-c)))).astype(BF16)
    o_ref[0] = _dot(s, w_ref[0].astype(BF16)) + b_ref[0]


def _ada_all(cs, ada_w, ada_b):
    rows = cs.shape[0]
    tn = 1536
    return pl.pallas_call(
        _ada_body,
        out_shape=jax.ShapeDtypeStruct((DEPTH, rows, 6 * D_MODEL), F32),
        grid=(DEPTH, 6 * D_MODEL // tn),
        in_specs=[
            pl.BlockSpec((rows, D_MODEL), lambda i, n: (0, 0)),
            pl.BlockSpec((1, D_MODEL, tn), lambda i, n: (i, 0, n)),
            pl.BlockSpec((1, 1, tn), lambda i, n: (i, 0, n)),
        ],
        out_specs=pl.BlockSpec((1, rows, tn), lambda i, n: (i, 0, n)),
        compiler_params=_params(40, 2),
        name="ada_mod",
    )(cs, ada_w, ada_b.reshape(DEPTH, 1, 6 * D_MODEL))


def _tok_spec(width, tm=ROW_TILE):
    return pl.BlockSpec((1, tm, width), lambda b, t: (b, t, 0))


def _mod_spec():
    return pl.BlockSpec((1, 6, D_MODEL), lambda b, t: (b, 0, 0))


def _rope_spec(tm=ROW_TILE):
    return pl.BlockSpec((tm, LANES), lambda b, t: (t, 0))


PROJ_TILE = 1024
PROJ_CHAIN = 256


def _chains(ref):
    return [slice(r, r + PROJ_CHAIN) for r in range(0, ref.shape[1], PROJ_CHAIN)]


def _gqa_proj_body(*refs, rope):
    if rope:
        x_ref, m_ref, w_ref, qn_ref, kn_ref, cos_ref, sin_ref, q_ref, k_ref, v_ref = refs
    else:
        x_ref, m_ref, w_ref, qn_ref, kn_ref, q_ref, k_ref, v_ref = refs
    hd = GQA_HEAD_DIM
    gains = (qn_ref[...] * hd ** -0.5, kn_ref[...])
    for rows in _chains(x_ref):
        h = _modulate(x_ref[0, rows, :], m_ref, 0, 1).astype(BF16)
        qkv = _dot(h, w_ref[...])
        if rope:
            cos, sin = cos_ref[rows, :], sin_ref[rows, :]
            mix = [(cos * g, sin * _rot_half(g)) for g in gains]
        for i in range(GQA_Q_HEADS + GQA_KV_HEADS):
            is_q = i < GQA_Q_HEADS
            t = qkv[:, i * hd:(i + 1) * hd]
            t = t * lax.rsqrt(jnp.mean(t * t, -1, keepdims=True) + RMS_EPS)
            if rope:
                a, b = mix[0 if is_q else 1]
                t = t * a + _rot_half(t) * b
            else:
                t = t * gains[0 if is_q else 1]
            if is_q:
                q_ref[0, rows, i * hd:(i + 1) * hd] = t.astype(BF16)
            else:
                j = i - GQA_Q_HEADS
                k_ref[0, rows, j * hd:(j + 1) * hd] = t.astype(BF16)
        v_ref[0, rows, :] = qkv[:, (GQA_Q_HEADS + GQA_KV_HEADS) * hd:].astype(BF16)


def _gqa_proj(x, mod, w, qn, kn, tables):
    bx, sx, _ = x.shape
    rope = tables is not None
    nk = GQA_KV_HEADS * GQA_HEAD_DIM
    tm = min(PROJ_TILE, sx)
    ins = [x, mod, w, qn, kn]
    specs = [_tok_spec(D_MODEL, tm), _mod_spec(), _full_spec(w, 2), _full_spec(qn, 2),
             _full_spec(kn, 2)]
    if rope:
        ins += list(tables)
        specs += [_rope_spec(tm), _rope_spec(tm)]
    return pl.pallas_call(
        functools.partial(_gqa_proj_body, rope=rope),
        out_shape=(
            jax.ShapeDtypeStruct((bx, sx, D_MODEL), BF16),
            jax.ShapeDtypeStruct((bx, sx, nk), BF16),
            jax.ShapeDtypeStruct((bx, sx, nk), BF16),
        ),
        grid=(bx, sx // tm),
        in_specs=specs,
        out_specs=(_tok_spec(D_MODEL, tm), _tok_spec(nk, tm), _tok_spec(nk, tm)),
        compiler_params=_params(48, 2),
        name="gqa_proj",
    )(*ins)


def _mla_proj_body(*refs, rope, need_q):
    refs = list(refs)
    x_ref, m_ref = refs[:2]
    del refs[:2]
    if need_q:
        wdq_ref, qn_ref, wuq_ref = refs[:3]
        del refs[:3]
    wdkv_ref, kvn_ref, wukv_ref = refs[:3]
    del refs[:3]
    if rope:
        cos_ref, sin_ref = refs[:2]
        del refs[:2]
    if need_q:
        q_ref = refs.pop(0)
    k_ref, v_ref = refs

    scale = (MLA_NOPE + MLA_ROPE) ** -0.5
    pw = MLA_QK_PAD
    for rows in _chains(x_ref):
        h = _modulate(x_ref[0, rows, :], m_ref, 0, 1).astype(BF16)
        if rope:
            cos, sin = cos_ref[rows, :], sin_ref[rows, :]
            cos_q, sin_q = cos * scale, sin * scale

        if need_q:
            cq = _rms(_dot(h, wdq_ref[...]), qn_ref[...]).astype(BF16)
            q = _dot(cq, wuq_ref[...])
            for i in range(MLA_HEADS):
                nope = q[:, i * pw:i * pw + MLA_NOPE]
                pe = q[:, i * pw + MLA_NOPE:(i + 1) * pw]
                pe = pe * cos_q + _rot_half(pe) * sin_q if rope else pe * scale
                q_ref[0, rows, i * pw:i * pw + MLA_NOPE] = (nope * scale).astype(BF16)
                q_ref[0, rows, i * pw + MLA_NOPE:(i + 1) * pw] = pe.astype(BF16)

        ckv = _dot(h, wdkv_ref[...])
        c_kv = _rms(ckv[:, :MLA_KV_RANK], kvn_ref[...]).astype(BF16)
        k_pe = ckv[:, MLA_KV_RANK:]
        if rope:
            k_pe = k_pe * cos + _rot_half(k_pe) * sin
        k_pe = k_pe.astype(BF16)
        kv = _dot(c_kv, wukv_ref[...])
        for i in range(MLA_HEADS):
            k_ref[0, rows, i * pw:i * pw + MLA_NOPE] = (
                kv[:, i * MLA_NOPE:(i + 1) * MLA_NOPE].astype(BF16))
            k_ref[0, rows, i * pw + MLA_NOPE:(i + 1) * pw] = k_pe
        v_ref[0, rows, :] = kv[:, MLA_HEADS * MLA_NOPE:].astype(BF16)


def _mla_proj(x, mod, wq, wkv, tables, need_q):
    bx, sx, _ = x.shape
    rope = tables is not None
    tm = min(PROJ_TILE, sx)
    ins = [x, mod]
    specs = [_tok_spec(D_MODEL, tm), _mod_spec()]
    weights = (list(wq) if need_q else []) + list(wkv)
    ins += weights
    specs += [_full_spec(w, 2) for w in weights]
    if rope:
        ins += list(tables)
        specs += [_rope_spec(tm), _rope_spec(tm)]
    kw = MLA_HEADS * MLA_QK_PAD
    vw = MLA_HEADS * MLA_V
    shapes = [jax.ShapeDtypeStruct((bx, sx, kw), BF16), jax.ShapeDtypeStruct((bx, sx, vw), BF16)]
    ospecs = [_tok_spec(kw, tm), _tok_spec(vw, tm)]
    if need_q:
        shapes.insert(0, jax.ShapeDtypeStruct((bx, sx, kw), BF16))
        ospecs.insert(0, _tok_spec(kw, tm))
    out = pl.pallas_call(
        functools.partial(_mla_proj_body, rope=rope, need_q=need_q),
        out_shape=tuple(shapes),
        grid=(bx, sx // tm),
        in_specs=specs,
        out_specs=tuple(ospecs),
        compiler_params=_params(48, 2),
        name="mla_proj",
    )(*ins)
    return out if need_q else (None,) + tuple(out)


def _na_proj_body(x_ref, m_ref, w_ref, b_ref, q_ref, k_ref, v_ref):
    scale = NA_HEAD_DIM ** -0.5
    for rows in _chains(x_ref):
        h = _modulate(x_ref[0, rows, :], m_ref, 0, 1).astype(BF16)
        qkv = _dot(h, w_ref[...]) + b_ref[...]
        q_ref[0, rows, :] = (qkv[:, :D_MODEL] * scale).astype(BF16)
        k_ref[0, rows, :] = qkv[:, D_MODEL:2 * D_MODEL].astype(BF16)
        v_ref[0, rows, :] = qkv[:, 2 * D_MODEL:].astype(BF16)


def _na_proj(x, mod, w, b):
    bx, sx, _ = x.shape
    tm = min(PROJ_TILE, sx)
    shape = jax.ShapeDtypeStruct((bx, sx, D_MODEL), BF16)
    return pl.pallas_call(
        _na_proj_body,
        out_shape=(shape, shape, shape),
        grid=(bx, sx // tm),
        in_specs=[_tok_spec(D_MODEL, tm), _mod_spec(), _full_spec(w, 2), _full_spec(b, 2)],
        out_specs=(_tok_spec(D_MODEL, tm),) * 3,
        compiler_params=_params(56, 2),
        name="na_proj",
    )(x, mod, w, b)


def _softmax_pv(q, keys, values, biases):
    scores = [_dot_nt(q, k) if b is None else _dot_nt(q, k) + b for k, b in zip(keys, biases)]
    m = functools.reduce(jnp.maximum, [jnp.max(s, -1, keepdims=True) for s in scores])
    probs = [jnp.exp(s - m) for s in scores]
    denom = functools.reduce(jnp.add, [jnp.sum(p, -1, keepdims=True) for p in probs])
    out = functools.reduce(jnp.add, [_dot(p.astype(BF16), v) for p, v in zip(probs, values)])
    return out * (1.0 / denom)


def _attn_body(*refs, heads, group, dk, dv, has_lat, chain):
    if has_lat:
        q_ref, kc_ref, vc_ref, k_ref, v_ref, o_ref = refs
    else:
        q_ref, kc_ref, vc_ref, o_ref = refs
    tq = q_ref.shape[1]
    for r in range(tq // chain):
        rows = slice(r * chain, (r + 1) * chain)
        outs = []
        for i in range(heads):
            keys = [kc_ref[0, :, i * dk:(i + 1) * dk]]
            values = [vc_ref[0, :, i * dv:(i + 1) * dv]]
            if has_lat:
                keys.append(k_ref[0, :, i * dk:(i + 1) * dk])
                values.append(v_ref[0, :, i * dv:(i + 1) * dv])
            for g in range(group):
                c0 = (i * group + g) * dk
                outs.append(_softmax_pv(q_ref[0, rows, c0:c0 + dk], keys, values, [None] * len(keys)))
        o_ref[0, rows, :] = jnp.concatenate(outs, -1).astype(o_ref.dtype)


def _attention(q, kc, vc, k, v, *, heads, group, dk, dv, tq, chain=256):
    b, sq, qw = q.shape
    has_lat = k is not None
    n_blocks = qw // (heads * group * dk)
    qb, kb, vb = heads * group * dk, heads * dk, heads * dv
    ins = [q, kc, vc]
    specs = [
        pl.BlockSpec((1, tq, qb), lambda bi, hi, ti: (bi, ti, hi)),
        pl.BlockSpec((1, kc.shape[1], kb), lambda bi, hi, ti: (bi, 0, hi)),
        pl.BlockSpec((1, vc.shape[1], vb), lambda bi, hi, ti: (bi, 0, hi)),
    ]
    if has_lat:
        ins += [k, v]
        specs += [
            pl.BlockSpec((1, k.shape[1], kb), lambda bi, hi, ti: (bi, 0, hi)),
            pl.BlockSpec((1, v.shape[1], vb), lambda bi, hi, ti: (bi, 0, hi)),
        ]
    ob = heads * group * dv
    return pl.pallas_call(
        functools.partial(_attn_body, heads=heads, group=group, dk=dk, dv=dv, has_lat=has_lat,
                          chain=min(chain, tq)),
        out_shape=jax.ShapeDtypeStruct((b, sq, n_blocks * ob), BF16),
        grid=(b, n_blocks, sq // tq),
        in_specs=specs,
        out_specs=pl.BlockSpec((1, tq, ob), lambda bi, hi, ti: (bi, ti, hi)),
        compiler_params=_params(48, 3),
        name="attn_lat" if has_lat else "attn_ctx",
    )(*ins)


def _na_window(j):
    n_blocks = N_ROWS // NA_Q_ROWS
    w0 = min(max(NA_Q_ROWS * j - NA_WIN_ROWS // 2, 0), N_ROWS - NA_K_ROWS)
    table = 0 if j == 0 else (2 if j == n_blocks - 1 else 1)
    return w0, table


def _na_bias_tables(rpb):
    n_blocks = N_ROWS // NA_Q_ROWS
    c = jnp.arange(GRID_W, dtype=jnp.int32)
    col_start = jnp.clip(c - NA_WIN_COLS // 2, 0, GRID_W - NA_WIN_COLS)
    col_ok = (c[None, :] >= col_start[:, None]) & (c[None, :] < col_start[:, None] + NA_WIN_COLS)
    edge = GRID_W - NA_WIN_COLS
    ext = jnp.pad(rpb, ((0, 0), (0, 0), (edge, edge)), mode="edge")
    slab = jnp.stack([ext[:, :, GRID_W - 1 - qc:2 * GRID_W - 1 - qc] for qc in range(GRID_W)], 1)
    slab = jnp.where(col_ok[None, :, None, :], slab, -jnp.inf)
    masked = jnp.full((NA_HEADS, GRID_W, GRID_W), -jnp.inf, F32)
    tables = []
    for j in (0, 1, n_blocks - 1):
        w0, _ = _na_window(j)
        q_rows = []
        for a in range(NA_Q_ROWS):
            r = NA_Q_ROWS * j + a
            rs = min(max(r - NA_WIN_ROWS // 2, 0), N_ROWS - NA_WIN_ROWS)
            pieces = []
            for kr in range(w0, w0 + NA_K_ROWS):
                in_window = rs <= kr < rs + NA_WIN_ROWS
                pieces.append(slab[:, :, kr - r + NA_WIN_ROWS - 1, :] if in_window else masked)
            q_rows.append(jnp.concatenate(pieces, -1))
        tables.append(jnp.concatenate(q_rows, 1))
    return jnp.stack(tables)


def _na_attn_body(q_ref, kc_ref, vc_ref, k_ref, v_ref, bias_ref, o_ref):
    hd = NA_HEAD_DIM
    heads = LANES // hd
    qn = NA_Q_ROWS * GRID_W
    for j in range(N_ROWS // NA_Q_ROWS):
        w0, table = _na_window(j)
        krows = slice(w0 * GRID_W, (w0 + NA_K_ROWS) * GRID_W)
        outs = []
        for i in range(heads):
            cols = slice(i * hd, (i + 1) * hd)
            outs.append(_softmax_pv(
                q_ref[0, j * qn:(j + 1) * qn, cols],
                [kc_ref[0, :, cols], k_ref[0, krows, cols]],
                [vc_ref[0, :, cols], v_ref[0, krows, cols]],
                [None, bias_ref[table, i]],
            ))
        o_ref[0, j * qn:(j + 1) * qn, :] = jnp.concatenate(outs, -1).astype(o_ref.dtype)


def _na_attention(q, kc, vc, k, v, bias):
    b, s, _ = q.shape
    n_pairs = D_MODEL // LANES
    heads = LANES // NA_HEAD_DIM
    lat = pl.BlockSpec((1, s, LANES), lambda hi, bi: (bi, 0, hi))
    ctx = pl.BlockSpec((1, CTX_LEN, LANES), lambda hi, bi: (bi, 0, hi))
    bias_spec = pl.BlockSpec((3, heads) + bias.shape[2:], lambda hi, bi: (0, hi, 0, 0))
    return pl.pallas_call(
        _na_attn_body,
        out_shape=jax.ShapeDtypeStruct((b, s, D_MODEL), BF16),
        grid=(n_pairs, b),
        in_specs=[lat, ctx, ctx, lat, lat, bias_spec],
        out_specs=lat,
        compiler_params=_params(48, 2),
        name="na_attn",
    )(q, kc, vc, k, v, bias)


MLP_CHUNK = 1024


def _tail_body(a_ref, wo_ref, x_ref, m_ref, g1_ref, b1_ref, w1_ref, w2_ref, g2_ref, b2_ref, o_ref):
    y = _dot(a_ref[0], wo_ref[...])
    x1 = _layer_norm(ALPHA * x_ref[0] + m_ref[0, 2:3, :] * y, g1_ref[...], b1_ref[...])
    h = _modulate(x1, m_ref, 3, 4).astype(BF16)
    acc = None
    for c in range(D_FF // MLP_CHUNK):
        cols = slice(c * MLP_CHUNK, (c + 1) * MLP_CHUNK)
        u = jnp.square(jnp.maximum(_dot(h, w1_ref[:, cols]), 0.0)).astype(BF16)
        part = _dot(u, w2_ref[cols, :])
        acc = part if acc is None else acc + part
    z = ALPHA * x1 + m_ref[0, 5:6, :] * acc
    o_ref[0] = _layer_norm(z, g2_ref[...], b2_ref[...])


def _layer_tail(a, wo, x, mod, g1, b1, w1, w2, g2, b2):
    bx, sx, _ = x.shape
    small = [_full_spec(g1, 2)] * 2
    return pl.pallas_call(
        _tail_body,
        out_shape=jax.ShapeDtypeStruct(x.shape, F32),
        grid=(bx, sx // ROW_TILE),
        in_specs=[_tok_spec(a.shape[-1]), _full_spec(wo, 2, single=True), _tok_spec(D_MODEL),
                  _mod_spec(), *small, _full_spec(w1, 2, single=True),
                  _full_spec(w2, 2, single=True), *small],
        out_specs=_tok_spec(D_MODEL),
        compiler_params=_params(56, 2),
        name="layer_tail",
    )(a, wo, x, mod, g1, b1, w1, w2, g2, b2)


def _row(v):
    return v.reshape(1, -1)


def kernel(x, c, ctx, c_ctx, ada_w, ada_b, ln1_g, ln1_b, ln2_g, ln2_b, mlp_w1, mlp_w2, mla_w_dq, mla_q_norm, mla_w_uq, mla_w_dkv, mla_kv_norm, mla_w_ukv, mla_w_o, na_w_qkv, na_b_qkv, na_rpb, na_w_o, gqa_w_qkv, gqa_q_norm, gqa_k_norm, gqa_w_o):
    pad_rows = 24 - BATCH - 1
    cs = jnp.concatenate([c, c_ctx[None, :], jnp.zeros((pad_rows, D_MODEL), F32)], 0)
    mod_all = _ada_all(cs, ada_w, ada_b).reshape(DEPTH, 24, 6, D_MODEL)

    xc = ctx.reshape(1, BATCH * CTX_LEN, D_MODEL)
    gqa_tables = _rope_tables(GQA_HEAD_DIM)
    mla_tables = _rope_tables(MLA_ROPE)

    def per_batch(t):
        return t.reshape(BATCH, CTX_LEN, t.shape[-1])

    for i in range(DEPTH):
        kind, j = i % N_MIXERS, i // N_MIXERS
        need_ctx = i < DEPTH - 1
        mod = mod_all[i, :BATCH]
        mod_c = mod_all[i, BATCH:BATCH + 1]

        if kind == 0:
            wuq = mla_w_uq[j].reshape(MLA_Q_RANK, MLA_HEADS, MLA_NOPE + MLA_ROPE)
            wuq = jnp.concatenate(
                [wuq[..., :MLA_NOPE], _split_rotary(wuq[..., MLA_NOPE:], MLA_ROPE)], -1)
            wq = (mla_w_dq[j].astype(BF16), _row(mla_q_norm[j]),
                  wuq.reshape(MLA_Q_RANK, MLA_HEADS * MLA_QK_PAD).astype(BF16))
            wdkv = mla_w_dkv[j]
            wdkv = jnp.concatenate(
                [wdkv[:, :MLA_KV_RANK], _split_rotary(wdkv[:, MLA_KV_RANK:], MLA_ROPE)], -1)
            wukv = mla_w_ukv[j].reshape(MLA_KV_RANK, MLA_HEADS, 2, MLA_NOPE)
            wukv = wukv.transpose(0, 2, 1, 3).reshape(MLA_KV_RANK, 2 * MLA_HEADS * MLA_NOPE)
            wkv = (wdkv.astype(BF16), _row(mla_kv_norm[j]), wukv.astype(BF16))
            q, k, v = _mla_proj(x, mod, wq, wkv, mla_tables, True)
            qc, kc, vc = _mla_proj(xc, mod_c, wq, wkv, None, need_ctx)
            kc, vc = per_batch(kc), per_batch(vc)
            dims = dict(heads=1, group=1, dk=MLA_QK_PAD, dv=MLA_V)
            y = _attention(q, kc, vc, k, v, tq=SEQ, **dims)
            if need_ctx:
                yc = _attention(per_batch(qc), kc, vc, None, None, tq=CTX_LEN, **dims)
            w_o = mla_w_o[j]
        elif kind == 1:
            w, b = na_w_qkv[j].astype(BF16), _row(na_b_qkv[j])
            q, k, v = _na_proj(x, mod, w, b)
            qc, kc, vc = _na_proj(xc, mod_c, w, b)
            qc, kc, vc = per_batch(qc), per_batch(kc), per_batch(vc)
            y = _na_attention(q, kc, vc, k, v, _na_bias_tables(na_rpb[j]))
            if need_ctx:
                yc = _attention(qc, kc, vc, None, None, tq=CTX_LEN, heads=LANES // NA_HEAD_DIM,
                                group=1, dk=NA_HEAD_DIM, dv=NA_HEAD_DIM)
            w_o = na_w_o[j]
        else:
            n_rot = (GQA_Q_HEADS + GQA_KV_HEADS) * GQA_HEAD_DIM
            w = gqa_w_qkv[j]
            w_rot = w[:, :n_rot].reshape(D_MODEL, GQA_Q_HEADS + GQA_KV_HEADS, GQA_HEAD_DIM)
            w_rot = _split_rotary(w_rot, GQA_HEAD_DIM).reshape(D_MODEL, n_rot)
            w = jnp.concatenate([w_rot, w[:, n_rot:]], -1).astype(BF16)
            qn = _row(_split_rotary(gqa_q_norm[j], GQA_HEAD_DIM))
            kn = _row(_split_rotary(gqa_k_norm[j], GQA_HEAD_DIM))
            q, k, v = _gqa_proj(x, mod, w, qn, kn, gqa_tables)
            qc, kc, vc = _gqa_proj(xc, mod_c, w, qn, kn, None)
            qc, kc, vc = per_batch(qc), per_batch(kc), per_batch(vc)
            dims = dict(heads=1, group=GQA_GROUP, dk=GQA_HEAD_DIM, dv=GQA_HEAD_DIM)
            y = _attention(q, kc, vc, k, v, tq=512, **dims)
            if need_ctx:
                yc = _attention(qc, kc, vc, None, None, tq=CTX_LEN, **dims)
            w_o = gqa_w_o[j]

        w_o = w_o.astype(BF16)
        w1, w2 = mlp_w1[i].astype(BF16), mlp_w2[i].astype(BF16)
        g1, b1, g2, b2 = _row(ln1_g[i]), _row(ln1_b[i]), _row(ln2_g[i]), _row(ln2_b[i])
        x = _layer_tail(y, w_o, x, mod, g1, b1, w1, w2, g2, b2)
        if need_ctx:
            yc = yc.reshape(1, BATCH * CTX_LEN, D_MODEL)
            xc = _layer_tail(yc, w_o, xc, mod_c, g1, b1, w1, w2, g2, b2)
    return x
```

```python
import functools

import jax
import jax.numpy as jnp
from jax import lax
from jax.experimental import pallas as pl
from jax.experimental.pallas import tpu as pltpu

D_MODEL = 1024
BATCH = 16
SEQ = 2048
DEPTH = 4
GRID_W = 64
CTX_LEN = 256
N_MIXERS = 3
ROPE_THETA = 10000.0
LN_EPS = 1e-5
RMS_EPS = 1e-6
D_FF = 4 * D_MODEL

MLA_HEADS = 8
MLA_Q_RANK = 512
MLA_KV_RANK = 256
MLA_NOPE = 128
MLA_ROPE = 64
MLA_V = 128
MLA_QK_PAD = 256

NA_HEADS = 16
NA_HEAD_DIM = D_MODEL // NA_HEADS
NA_WIN_ROWS = 8
NA_WIN_COLS = 16
NA_Q_ROWS = 4
NA_K_ROWS = 12
N_ROWS = SEQ // GRID_W

GQA_HEAD_DIM = 128
GQA_Q_HEADS = D_MODEL // GQA_HEAD_DIM
GQA_KV_HEADS = GQA_Q_HEADS // 4
GQA_GROUP = GQA_Q_HEADS // GQA_KV_HEADS

ALPHA = (2.0 * DEPTH) ** 0.25
LANES = 128
ROW_TILE = 512
F32 = jnp.float32
BF16 = jnp.bfloat16
MIB = 1024 * 1024


def _params(vmem_mib, n_axes):
    return pltpu.CompilerParams(
        dimension_semantics=("arbitrary",) * n_axes, vmem_limit_bytes=vmem_mib * MIB
    )


def _full_spec(arr, n_axes, single=False):
    zeros = (0,) * arr.ndim
    index_map = {2: lambda a, b: zeros, 3: lambda a, b, c: zeros}[n_axes]
    if single:
        return pl.BlockSpec(arr.shape, index_map, pipeline_mode=pl.Buffered(1))
    return pl.BlockSpec(arr.shape, index_map)


def _dot(a, b):
    return jnp.dot(a, b, preferred_element_type=F32)


def _dot_nt(a, b):
    return lax.dot_general(a, b, (((1,), (1,)), ((), ())), preferred_element_type=F32)


def _modulate(x, m_ref, shift_row, scale_row):
    shift = m_ref[0, shift_row:shift_row + 1, :]
    scale = m_ref[0, scale_row:scale_row + 1, :]
    return x * (1.0 + scale) + shift


def _rms(t, g):
    return t * lax.rsqrt(jnp.mean(t * t, -1, keepdims=True) + RMS_EPS) * g


def _layer_norm(z, g, b):
    mu = jnp.mean(z, -1, keepdims=True)
    zc = z - mu
    var = jnp.mean(zc * zc, -1, keepdims=True)
    return zc * lax.rsqrt(var + LN_EPS) * g + b


HALF_LANES = LANES // 2


def _rot_half(t):
    return pltpu.roll(t, HALF_LANES, 1)


def _split_rotary(t, d):
    lead = t.shape[:-1]
    t = t.reshape(lead + (2, 2, d // 4)).swapaxes(-3, -2).reshape(lead + (2, d // 2))
    t = jnp.pad(t, [(0, 0)] * (len(lead) + 1) + [(0, HALF_LANES - d // 2)])
    return t.reshape(lead + (LANES,))


def _rope_tables(d):
    t = jnp.arange(SEQ, dtype=jnp.int32)
    rows, cols = t // GRID_W, t % GRID_W
    quarter = d // 4
    freqs = ROPE_THETA ** (-jnp.arange(quarter, dtype=F32) / quarter)
    ang = jnp.concatenate([rows.astype(F32)[:, None] * freqs, cols.astype(F32)[:, None] * freqs], -1)
    c, s = jnp.cos(ang), jnp.sin(ang)
    pad = jnp.zeros((SEQ, HALF_LANES - d // 2), F32)
    return jnp.concatenate([c, pad, c, pad], -1), jnp.concatenate([-s, pad, s, pad], -1)


def _ada_body(c_ref, w_ref, b_ref, o_ref):
    c = c_ref[...]
    s = (c * (1.0 / (1.0 + jnp.exp(-c)))).astype(BF16)
    o_ref[0] = _dot(s, w_ref[0].astype(BF16)) + b_ref[0]


def _ada_all(cs, ada_w, ada_b):
    rows = cs.shape[0]
    tn = 1536
    return pl.pallas_call(
        _ada_body,
        out_shape=jax.ShapeDtypeStruct((DEPTH, rows, 6 * D_MODEL), F32),
        grid=(DEPTH, 6 * D_MODEL // tn),
        in_specs=[
            pl.BlockSpec((rows, D_MODEL), lambda i, n: (0, 0)),
            pl.BlockSpec((1, D_MODEL, tn), lambda i, n: (i, 0, n)),
            pl.BlockSpec((1, 1, tn), lambda i, n: (i, 0, n)),
        ],
        out_specs=pl.BlockSpec((1, rows, tn), lambda i, n: (i, 0, n)),
        compiler_params=_params(40, 2),
        name="ada_mod",
    )(cs, ada_w, ada_b.reshape(DEPTH, 1, 6 * D_MODEL))


def _tok_spec(width, tm=ROW_TILE):
    return pl.BlockSpec((1, tm, width), lambda b, t: (b, t, 0))


def _mod_spec():
    return pl.BlockSpec((1, 6, D_MODEL), lambda b, t: (b, 0, 0))


def _rope_spec(tm=ROW_TILE):
    return pl.BlockSpec((tm, LANES), lambda b, t: (t, 0))


PROJ_TILE = 1024
PROJ_CHAIN = 256
MLA_PROJ_CHAIN = 512


def _chains(ref, chain=PROJ_CHAIN):
    return [slice(r, r + chain) for r in range(0, ref.shape[1], chain)]


def _gqa_proj_body(*refs, rope):
    if rope:
        x_ref, m_ref, w_ref, qn_ref, kn_ref, cos_ref, sin_ref, q_ref, k_ref, v_ref = refs
    else:
        x_ref, m_ref, w_ref, qn_ref, kn_ref, q_ref, k_ref, v_ref = refs
    hd = GQA_HEAD_DIM
    gains = (qn_ref[...] * (hd ** -0.5 * LOG2E), kn_ref[...])
    for rows in _chains(x_ref):
        h = _modulate(x_ref[0, rows, :], m_ref, 0, 1).astype(BF16)
        qkv = _dot(h, w_ref[...])
        if rope:
            cos, sin = cos_ref[rows, :], sin_ref[rows, :]
            mix = [(cos * g, sin * _rot_half(g)) for g in gains]
        for i in range(GQA_Q_HEADS + GQA_KV_HEADS):
            is_q = i < GQA_Q_HEADS
            t = qkv[:, i * hd:(i + 1) * hd]
            t = t * lax.rsqrt(jnp.mean(t * t, -1, keepdims=True) + RMS_EPS)
            if rope:
                a, b = mix[0 if is_q else 1]
                t = t * a + _rot_half(t) * b
            else:
                t = t * gains[0 if is_q else 1]
            if is_q:
                q_ref[0, rows, i * hd:(i + 1) * hd] = t.astype(BF16)
            else:
                j = i - GQA_Q_HEADS
                k_ref[0, rows, j * hd:(j + 1) * hd] = t.astype(BF16)
        v_ref[0, rows, :] = qkv[:, (GQA_Q_HEADS + GQA_KV_HEADS) * hd:].astype(BF16)


def _gqa_proj(x, mod, w, qn, kn, tables):
    bx, sx, _ = x.shape
    rope = tables is not None
    nk = GQA_KV_HEADS * GQA_HEAD_DIM
    tm = min(PROJ_TILE, sx)
    ins = [x, mod, w, qn, kn]
    specs = [_tok_spec(D_MODEL, tm), _mod_spec(), _full_spec(w, 2), _full_spec(qn, 2),
             _full_spec(kn, 2)]
    if rope:
        ins += list(tables)
        specs += [_rope_spec(tm), _rope_spec(tm)]
    return pl.pallas_call(
        functools.partial(_gqa_proj_body, rope=rope),
        out_shape=(
            jax.ShapeDtypeStruct((bx, sx, D_MODEL), BF16),
            jax.ShapeDtypeStruct((bx, sx, nk), BF16),
            jax.ShapeDtypeStruct((bx, sx, nk), BF16),
        ),
        grid=(bx, sx // tm),
        in_specs=specs,
        out_specs=(_tok_spec(D_MODEL, tm), _tok_spec(nk, tm), _tok_spec(nk, tm)),
        compiler_params=_params(48, 2),
        name="gqa_proj",
    )(*ins)


def _mla_proj_body(*refs, rope, need_q):
    refs = list(refs)
    x_ref, m_ref = refs[:2]
    del refs[:2]
    if need_q:
        wdq_ref, qn_ref, wuq_ref = refs[:3]
        del refs[:3]
    wdkv_ref, kvn_ref, wukv_ref = refs[:3]
    del refs[:3]
    if rope:
        cos_ref, sin_ref = refs[:2]
        del refs[:2]
    if need_q:
        q_ref = refs.pop(0)
    k_ref, v_ref = refs

    scale = (MLA_NOPE + MLA_ROPE) ** -0.5 * LOG2E
    pw = MLA_QK_PAD
    for rows in _chains(x_ref, MLA_PROJ_CHAIN):
        h = _modulate(x_ref[0, rows, :], m_ref, 0, 1).astype(BF16)
        if rope:
            cos, sin = cos_ref[rows, :], sin_ref[rows, :]
            cos_q, sin_q = cos * scale, sin * scale

        if need_q:
            cq = _rms(_dot(h, wdq_ref[...]), qn_ref[...]).astype(BF16)
            q = _dot(cq, wuq_ref[...])
            for i in range(MLA_HEADS):
                nope = q[:, i * pw:i * pw + MLA_NOPE]
                pe = q[:, i * pw + MLA_NOPE:(i + 1) * pw]
                pe = pe * cos_q + _rot_half(pe) * sin_q if rope else pe * scale
                q_ref[0, rows, i * pw:i * pw + MLA_NOPE] = (nope * scale).astype(BF16)
                q_ref[0, rows, i * pw + MLA_NOPE:(i + 1) * pw] = pe.astype(BF16)

        ckv = _dot(h, wdkv_ref[...])
        c_kv = _rms(ckv[:, :MLA_KV_RANK], kvn_ref[...]).astype(BF16)
        k_pe = ckv[:, MLA_KV_RANK:]
        if rope:
            k_pe = k_pe * cos + _rot_half(k_pe) * sin
        k_pe = k_pe.astype(BF16)
        kv = _dot(c_kv, wukv_ref[...])
        for i in range(MLA_HEADS):
            k_ref[0, rows, i * pw:i * pw + MLA_NOPE] = (
                kv[:, i * MLA_NOPE:(i + 1) * MLA_NOPE].astype(BF16))
            k_ref[0, rows, i * pw + MLA_NOPE:(i + 1) * pw] = k_pe
        v_ref[0, rows, :] = kv[:, MLA_HEADS * MLA_NOPE:].astype(BF16)


def _mla_proj(x, mod, wq, wkv, tables, need_q):
    bx, sx, _ = x.shape
    rope = tables is not None
    tm = min(PROJ_TILE, sx)
    ins = [x, mod]
    specs = [_tok_spec(D_MODEL, tm), _mod_spec()]
    weights = (list(wq) if need_q else []) + list(wkv)
    ins += weights
    specs += [_full_spec(w, 2) for w in weights]
    if rope:
        ins += list(tables)
        specs += [_rope_spec(tm), _rope_spec(tm)]
    kw = MLA_HEADS * MLA_QK_PAD
    vw = MLA_HEADS * MLA_V
    shapes = [jax.ShapeDtypeStruct((bx, sx, kw), BF16), jax.ShapeDtypeStruct((bx, sx, vw), BF16)]
    ospecs = [_tok_spec(kw, tm), _tok_spec(vw, tm)]
    if need_q:
        shapes.insert(0, jax.ShapeDtypeStruct((bx, sx, kw), BF16))
        ospecs.insert(0, _tok_spec(kw, tm))
    out = pl.pallas_call(
        functools.partial(_mla_proj_body, rope=rope, need_q=need_q),
        out_shape=tuple(shapes),
        grid=(bx, sx // tm),
        in_specs=specs,
        out_specs=tuple(ospecs),
        compiler_params=_params(48, 2),
        name="mla_proj",
    )(*ins)
    return out if need_q else (None,) + tuple(out)


def _na_proj_body(x_ref, m_ref, w_ref, b_ref, q_ref, k_ref, v_ref):
    scale = NA_HEAD_DIM ** -0.5 * LOG2E
    for rows in _chains(x_ref):
        h = _modulate(x_ref[0, rows, :], m_ref, 0, 1).astype(BF16)
        qkv = _dot(h, w_ref[...]) + b_ref[...]
        q_ref[0, rows, :] = (qkv[:, :D_MODEL] * scale).astype(BF16)
        k_ref[0, rows, :] = qkv[:, D_MODEL:2 * D_MODEL].astype(BF16)
        v_ref[0, rows, :] = qkv[:, 2 * D_MODEL:].astype(BF16)


def _na_proj(x, mod, w, b):
    bx, sx, _ = x.shape
    tm = min(PROJ_TILE, sx)
    shape = jax.ShapeDtypeStruct((bx, sx, D_MODEL), BF16)
    return pl.pallas_call(
        _na_proj_body,
        out_shape=(shape, shape, shape),
        grid=(bx, sx // tm),
        in_specs=[_tok_spec(D_MODEL, tm), _mod_spec(), _full_spec(w, 2), _full_spec(b, 2)],
        out_specs=(_tok_spec(D_MODEL, tm),) * 3,
        compiler_params=_params(56, 2),
        name="na_proj",
    )(x, mod, w, b)


LOG2E = 1.4426950408889634


def _augment(v_block, sub, dv):
    col = lax.broadcasted_iota(jnp.int32, v_block.shape, 1)
    if dv == LANES:
        return jnp.concatenate([v_block, jnp.where(col == 0, 1.0, 0.0).astype(BF16)], -1)
    mine = (col < dv) if sub == 0 else (col >= dv)
    ones = jnp.where(col == (dv if sub == 0 else 0), 1.0, 0.0)
    return jnp.where(mine, v_block.astype(F32), ones).astype(BF16)


def _own_lanes(q_block, sub, dk):
    col = lax.broadcasted_iota(jnp.int32, q_block.shape, 1)
    mine = (col >= sub * dk) & (col < (sub + 1) * dk)
    return jnp.where(mine, q_block.astype(F32), 0.0).astype(BF16)


def _normalised(acc, sub, dv):
    ones_col = LANES if dv == LANES else (dv if sub == 0 else 0)
    return acc[:, :LANES] * (1.0 / acc[:, ones_col:ones_col + 1])


def _softmax_pv(q, keys, values, biases):
    scores = [_dot_nt(q, k) if b is None else _dot_nt(q, k) + b for k, b in zip(keys, biases)]
    m = functools.reduce(jnp.maximum, [jnp.max(s, -1, keepdims=True) for s in scores])
    return functools.reduce(
        jnp.add, [_dot(jnp.exp2(s - m).astype(BF16), v) for s, v in zip(scores, values)])


def _store_heads(o_ref, rows, block, outs, dv):
    if dv == LANES:
        for g, o in enumerate(outs):
            c0 = (block * len(outs) + g) * LANES
            o_ref[0, rows, c0:c0 + LANES] = o.astype(o_ref.dtype)
    else:
        col = lax.broadcasted_iota(jnp.int32, outs[0].shape, 1)
        pair = jnp.where(col < dv, outs[0], outs[1])
        o_ref[0, rows, block * LANES:(block + 1) * LANES] = pair.astype(o_ref.dtype)


def _attn_body(*refs, heads, group, dk, dv, has_lat, chain):
    if has_lat:
        q_ref, kc_ref, vc_ref, k_ref, v_ref, o_ref = refs
    else:
        q_ref, kc_ref, vc_ref, o_ref = refs
    tq = q_ref.shape[1]
    per_block = LANES // dv
    assert per_block == 1 or group == 1
    value_refs = [vc_ref, v_ref] if has_lat else [vc_ref]
    key_refs = [kc_ref, k_ref] if has_lat else [kc_ref]
    values = [[_augment(ref[0, :, (i // per_block) * LANES:(i // per_block + 1) * LANES],
                        i % per_block, dv) for ref in value_refs] for i in range(heads)]
    for r in range(tq // chain):
        rows = slice(r * chain, (r + 1) * chain)
        for block in range(heads // per_block):
            outs = []
            for sub in range(per_block):
                i = block * per_block + sub
                keys = [ref[0, :, i * dk:(i + 1) * dk] for ref in key_refs]
                for g in range(group):
                    c0 = (i * group + g) * dk
                    acc = _softmax_pv(q_ref[0, rows, c0:c0 + dk], keys, values[i],
                                      [None] * len(keys))
                    outs.append(_normalised(acc, sub, dv))
            _store_heads(o_ref, rows, block, outs, dv)


def _attention(q, kc, vc, k, v, *, heads, group, dk, dv, tq, chain=256):
    b, sq, qw = q.shape
    has_lat = k is not None
    n_blocks = qw // (heads * group * dk)
    qb, kb, vb = heads * group * dk, heads * dk, heads * dv
    ins = [q, kc, vc]
    specs = [
        pl.BlockSpec((1, tq, qb), lambda bi, hi, ti: (bi, ti, hi)),
        pl.BlockSpec((1, kc.shape[1], kb), lambda bi, hi, ti: (bi, 0, hi)),
        pl.BlockSpec((1, vc.shape[1], vb), lambda bi, hi, ti: (bi, 0, hi)),
    ]
    if has_lat:
        ins += [k, v]
        specs += [
            pl.BlockSpec((1, k.shape[1], kb), lambda bi, hi, ti: (bi, 0, hi)),
            pl.BlockSpec((1, v.shape[1], vb), lambda bi, hi, ti: (bi, 0, hi)),
        ]
    ob = heads * group * dv
    return pl.pallas_call(
        functools.partial(_attn_body, heads=heads, group=group, dk=dk, dv=dv, has_lat=has_lat,
                          chain=min(chain, tq)),
        out_shape=jax.ShapeDtypeStruct((b, sq, n_blocks * ob), BF16),
        grid=(b, n_blocks, sq // tq),
        in_specs=specs,
        out_specs=pl.BlockSpec((1, tq, ob), lambda bi, hi, ti: (bi, ti, hi)),
        compiler_params=_params(48, 3),
        name="attn_lat" if has_lat else "attn_ctx",
    )(*ins)


def _na_window(j):
    n_blocks = N_ROWS // NA_Q_ROWS
    w0 = min(max(NA_Q_ROWS * j - NA_WIN_ROWS // 2, 0), N_ROWS - NA_K_ROWS)
    table = 0 if j == 0 else (2 if j == n_blocks - 1 else 1)
    return w0, table


def _na_bias_tables(rpb):
    n_blocks = N_ROWS // NA_Q_ROWS
    c = jnp.arange(GRID_W, dtype=jnp.int32)
    col_start = jnp.clip(c - NA_WIN_COLS // 2, 0, GRID_W - NA_WIN_COLS)
    col_ok = (c[None, :] >= col_start[:, None]) & (c[None, :] < col_start[:, None] + NA_WIN_COLS)
    edge = GRID_W - NA_WIN_COLS
    ext = jnp.pad(rpb, ((0, 0), (0, 0), (edge, edge)), mode="edge")
    slab = jnp.stack([ext[:, :, GRID_W - 1 - qc:2 * GRID_W - 1 - qc] for qc in range(GRID_W)], 1)
    slab = jnp.where(col_ok[None, :, None, :], slab * LOG2E, -jnp.inf)
    masked = jnp.full((NA_HEADS, GRID_W, GRID_W), -jnp.inf, F32)
    tables = []
    for j in (0, 1, n_blocks - 1):
        w0, _ = _na_window(j)
        q_rows = []
        for a in range(NA_Q_ROWS):
            r = NA_Q_ROWS * j + a
            rs = min(max(r - NA_WIN_ROWS // 2, 0), N_ROWS - NA_WIN_ROWS)
            pieces = []
            for kr in range(w0, w0 + NA_K_ROWS):
                in_window = rs <= kr < rs + NA_WIN_ROWS
                pieces.append(slab[:, :, kr - r + NA_WIN_ROWS - 1, :] if in_window else masked)
            q_rows.append(jnp.concatenate(pieces, -1))
        tables.append(jnp.concatenate(q_rows, 1))
    return jnp.stack(tables)


def _na_attn_body(q_ref, kc_ref, vc_ref, k_ref, v_ref, bias_ref, o_ref):
    hd = NA_HEAD_DIM
    heads = LANES // hd
    qn = NA_Q_ROWS * GRID_W
    vc_aug = [_augment(vc_ref[0], i, hd) for i in range(heads)]
    v_aug = [_augment(v_ref[0], i, hd) for i in range(heads)]
    for j in range(N_ROWS // NA_Q_ROWS):
        w0, table = _na_window(j)
        krows = slice(w0 * GRID_W, (w0 + NA_K_ROWS) * GRID_W)
        rows = slice(j * qn, (j + 1) * qn)
        outs = []
        for i in range(heads):
            acc = _softmax_pv(
                _own_lanes(q_ref[0, rows, :], i, hd),
                [kc_ref[0], k_ref[0, krows, :]],
                [vc_aug[i], v_aug[i][krows]],
                [None, bias_ref[table, i]],
            )
            outs.append(_normalised(acc, i, hd))
        _store_heads(o_ref, rows, 0, outs, hd)


def _na_attention(q, kc, vc, k, v, bias):
    b, s, _ = q.shape
    n_pairs = D_MODEL // LANES
    heads = LANES // NA_HEAD_DIM
    lat = pl.BlockSpec((1, s, LANES), lambda hi, bi: (bi, 0, hi))
    ctx = pl.BlockSpec((1, CTX_LEN, LANES), lambda hi, bi: (bi, 0, hi))
    bias_spec = pl.BlockSpec((3, heads) + bias.shape[2:], lambda hi, bi: (0, hi, 0, 0))
    return pl.pallas_call(
        _na_attn_body,
        out_shape=jax.ShapeDtypeStruct((b, s, D_MODEL), BF16),
        grid=(n_pairs, b),
        in_specs=[lat, ctx, ctx, lat, lat, bias_spec],
        out_specs=lat,
        compiler_params=_params(48, 2),
        name="na_attn",
    )(q, kc, vc, k, v, bias)


MLP_CHUNK = 1024


def _tail_body(a_ref, wo_ref, x_ref, m_ref, g1_ref, b1_ref, w1_ref, w2_ref, g2_ref, b2_ref, o_ref):
    y = _dot(a_ref[0], wo_ref[...])
    x1 = _layer_norm(ALPHA * x_ref[0] + m_ref[0, 2:3, :] * y, g1_ref[...], b1_ref[...])
    h = _modulate(x1, m_ref, 3, 4).astype(BF16)
    acc = None
    for c in range(D_FF // MLP_CHUNK):
        cols = slice(c * MLP_CHUNK, (c + 1) * MLP_CHUNK)
        u = jnp.square(jnp.maximum(_dot(h, w1_ref[:, cols]), 0.0)).astype(BF16)
        part = _dot(u, w2_ref[cols, :])
        acc = part if acc is None else acc + part
    z = ALPHA * x1 + m_ref[0, 5:6, :] * acc
    o_ref[0] = _layer_norm(z, g2_ref[...], b2_ref[...])


def _layer_tail(a, wo, x, mod, g1, b1, w1, w2, g2, b2):
    bx, sx, _ = x.shape
    small = [_full_spec(g1, 2)] * 2
    return pl.pallas_call(
        _tail_body,
        out_shape=jax.ShapeDtypeStruct(x.shape, F32),
        grid=(bx, sx // ROW_TILE),
        in_specs=[_tok_spec(a.shape[-1]), _full_spec(wo, 2, single=True), _tok_spec(D_MODEL),
                  _mod_spec(), *small, _full_spec(w1, 2, single=True),
                  _full_spec(w2, 2, single=True), *small],
        out_specs=_tok_spec(D_MODEL),
        compiler_params=_params(56, 2),
        name="layer_tail",
    )(a, wo, x, mod, g1, b1, w1, w2, g2, b2)


def _row(v):
    return v.reshape(1, -1)


def kernel(x, c, ctx, c_ctx, ada_w, ada_b, ln1_g, ln1_b, ln2_g, ln2_b, mlp_w1, mlp_w2, mla_w_dq, mla_q_norm, mla_w_uq, mla_w_dkv, mla_kv_norm, mla_w_ukv, mla_w_o, na_w_qkv, na_b_qkv, na_rpb, na_w_o, gqa_w_qkv, gqa_q_norm, gqa_k_norm, gqa_w_o):
    pad_rows = 24 - BATCH - 1
    cs = jnp.concatenate([c, c_ctx[None, :], jnp.zeros((pad_rows, D_MODEL), F32)], 0)
    mod_all = _ada_all(cs, ada_w, ada_b).reshape(DEPTH, 24, 6, D_MODEL)

    xc = ctx.reshape(1, BATCH * CTX_LEN, D_MODEL)
    gqa_tables = _rope_tables(GQA_HEAD_DIM)
    mla_tables = _rope_tables(MLA_ROPE)

    def per_batch(t):
        return t.reshape(BATCH, CTX_LEN, t.shape[-1])

    for i in range(DEPTH):
        kind, j = i % N_MIXERS, i // N_MIXERS
        need_ctx = i < DEPTH - 1
        mod = mod_all[i, :BATCH]
        mod_c = mod_all[i, BATCH:BATCH + 1]

        if kind == 0:
            wuq = mla_w_uq[j].reshape(MLA_Q_RANK, MLA_HEADS, MLA_NOPE + MLA_ROPE)
            wuq = jnp.concatenate(
                [wuq[..., :MLA_NOPE], _split_rotary(wuq[..., MLA_NOPE:], MLA_ROPE)], -1)
            wq = (mla_w_dq[j].astype(BF16), _row(mla_q_norm[j]),
                  wuq.reshape(MLA_Q_RANK, MLA_HEADS * MLA_QK_PAD).astype(BF16))
            wdkv = mla_w_dkv[j]
            wdkv = jnp.concatenate(
                [wdkv[:, :MLA_KV_RANK], _split_rotary(wdkv[:, MLA_KV_RANK:], MLA_ROPE)], -1)
            wukv = mla_w_ukv[j].reshape(MLA_KV_RANK, MLA_HEADS, 2, MLA_NOPE)
            wukv = wukv.transpose(0, 2, 1, 3).reshape(MLA_KV_RANK, 2 * MLA_HEADS * MLA_NOPE)
            wkv = (wdkv.astype(BF16), _row(mla_kv_norm[j]), wukv.astype(BF16))
            q, k, v = _mla_proj(x, mod, wq, wkv, mla_tables, True)
            qc, kc, vc = _mla_proj(xc, mod_c, wq, wkv, None, need_ctx)
            kc, vc = per_batch(kc), per_batch(vc)
            dims = dict(heads=1, group=1, dk=MLA_QK_PAD, dv=MLA_V)
            y = _attention(q, kc, vc, k, v, tq=SEQ, **dims)
            if need_ctx:
                yc = _attention(per_batch(qc), kc, vc, None, None, tq=CTX_LEN,
                                **{**dims, "heads": MLA_HEADS})
            w_o = mla_w_o[j]
        elif kind == 1:
            w, b = na_w_qkv[j].astype(BF16), _row(na_b_qkv[j])
            q, k, v = _na_proj(x, mod, w, b)
            qc, kc, vc = _na_proj(xc, mod_c, w, b)
            qc, kc, vc = per_batch(qc), per_batch(kc), per_batch(vc)
            y = _na_attention(q, kc, vc, k, v, _na_bias_tables(na_rpb[j]))
            if need_ctx:
                yc = _attention(qc, kc, vc, None, None, tq=CTX_LEN, heads=NA_HEADS,
                                group=1, dk=NA_HEAD_DIM, dv=NA_HEAD_DIM)
            w_o = na_w_o[j]
        else:
            n_rot = (GQA_Q_HEADS + GQA_KV_HEADS) * GQA_HEAD_DIM
            w = gqa_w_qkv[j]
            w_rot = w[:, :n_rot].reshape(D_MODEL, GQA_Q_HEADS + GQA_KV_HEADS, GQA_HEAD_DIM)
            w_rot = _split_rotary(w_rot, GQA_HEAD_DIM).reshape(D_MODEL, n_rot)
            w = jnp.concatenate([w_rot, w[:, n_rot:]], -1).astype(BF16)
            qn = _row(_split_rotary(gqa_q_norm[j], GQA_HEAD_DIM))
            kn = _row(_split_rotary(gqa_k_norm[j], GQA_HEAD_DIM))
            q, k, v = _gqa_proj(x, mod, w, qn, kn, gqa_tables)
            qc, kc, vc = _gqa_proj(xc, mod_c, w, qn, kn, None)
            qc, kc, vc = per_batch(qc), per_batch(kc), per_batch(vc)
            dims = dict(heads=1, group=GQA_GROUP, dk=GQA_HEAD_DIM, dv=GQA_HEAD_DIM)
            y = _attention(q, kc, vc, k, v, tq=512, **dims)
            if need_ctx:
                yc = _attention(qc, kc, vc, None, None, tq=CTX_LEN,
                                **{**dims, "heads": GQA_KV_HEADS})
            w_o = gqa_w_o[j]

        w_o = w_o.astype(BF16)
        w1, w2 = mlp_w1[i].astype(BF16), mlp_w2[i].astype(BF16)
        g1, b1, g2, b2 = _row(ln1_g[i]), _row(ln1_b[i]), _row(ln2_g[i]), _row(ln2_b[i])
        x = _layer_tail(y, w_o, x, mod, g1, b1, w1, w2, g2, b2)
        if need_ctx:
            yc = yc.reshape(1, BATCH * CTX_LEN, D_MODEL)
            xc = _layer_tail(yc, w_o, xc, mod_c, g1, b1, w1, w2, g2, b2)
    return x
```

```python
import functools

import jax
import jax.numpy as jnp
from jax import lax
from jax.experimental import pallas as pl
from jax.experimental.pallas import tpu as pltpu

D_MODEL = 1024
BATCH = 16
SEQ = 2048
DEPTH = 4
GRID_W = 64
CTX_LEN = 256
N_MIXERS = 3
ROPE_THETA = 10000.0
LN_EPS = 1e-5
RMS_EPS = 1e-6
D_FF = 4 * D_MODEL

MLA_HEADS = 8
MLA_Q_RANK = 512
MLA_KV_RANK = 256
MLA_NOPE = 128
MLA_ROPE = 64
MLA_V = 128
MLA_QK_PAD = 256

NA_HEADS = 16
NA_HEAD_DIM = D_MODEL // NA_HEADS
NA_WIN_ROWS = 8
NA_WIN_COLS = 16
NA_Q_ROWS = 4
NA_K_ROWS = 12
N_ROWS = SEQ // GRID_W

GQA_HEAD_DIM = 128
GQA_Q_HEADS = D_MODEL // GQA_HEAD_DIM
GQA_KV_HEADS = GQA_Q_HEADS // 4
GQA_GROUP = GQA_Q_HEADS // GQA_KV_HEADS

ALPHA = (2.0 * DEPTH) ** 0.25
LANES = 128
ROW_TILE = 1024
F32 = jnp.float32
BF16 = jnp.bfloat16
MIB = 1024 * 1024


def _params(vmem_mib, n_axes):
    return pltpu.CompilerParams(
        dimension_semantics=("arbitrary",) * n_axes, vmem_limit_bytes=vmem_mib * MIB
    )


def _full_spec(arr, n_axes, single=False):
    zeros = (0,) * arr.ndim
    index_map = {2: lambda a, b: zeros, 3: lambda a, b, c: zeros}[n_axes]
    if single:
        return pl.BlockSpec(arr.shape, index_map, pipeline_mode=pl.Buffered(1))
    return pl.BlockSpec(arr.shape, index_map)


def _dot(a, b):
    return jnp.dot(a, b, preferred_element_type=F32)


def _dot_nt(a, b):
    return lax.dot_general(a, b, (((1,), (1,)), ((), ())), preferred_element_type=F32)


def _modulate(x, m_ref, shift_row, scale_row):
    shift = m_ref[0, shift_row:shift_row + 1, :]
    scale = m_ref[0, scale_row:scale_row + 1, :]
    return x * (1.0 + scale) + shift


def _rms(t, g):
    return t * lax.rsqrt(jnp.mean(t * t, -1, keepdims=True) + RMS_EPS) * g


def _layer_norm(z, g, b):
    mu = jnp.mean(z, -1, keepdims=True)
    zc = z - mu
    var = jnp.mean(zc * zc, -1, keepdims=True)
    return zc * lax.rsqrt(var + LN_EPS) * g + b


HALF_LANES = LANES // 2


def _rot_half(t):
    return pltpu.roll(t, HALF_LANES, 1)


def _split_rotary(t, d):
    lead = t.shape[:-1]
    t = t.reshape(lead + (2, 2, d // 4)).swapaxes(-3, -2).reshape(lead + (2, d // 2))
    t = jnp.pad(t, [(0, 0)] * (len(lead) + 1) + [(0, HALF_LANES - d // 2)])
    return t.reshape(lead + (LANES,))


def _rope_tables(d):
    t = jnp.arange(SEQ, dtype=jnp.int32)
    rows, cols = t // GRID_W, t % GRID_W
    quarter = d // 4
    freqs = ROPE_THETA ** (-jnp.arange(quarter, dtype=F32) / quarter)
    ang = jnp.concatenate([rows.astype(F32)[:, None] * freqs, cols.astype(F32)[:, None] * freqs], -1)
    c, s = jnp.cos(ang), jnp.sin(ang)
    pad = jnp.zeros((SEQ, HALF_LANES - d // 2), F32)
    return jnp.concatenate([c, pad, c, pad], -1), jnp.concatenate([-s, pad, s, pad], -1)


def _ada_body(c_ref, w_ref, b_ref, o_ref):
    c = c_ref[...]
    s = (c * (1.0 / (1.0 + jnp.exp(-c)))).astype(BF16)
    o_ref[0] = _dot(s, w_ref[0].astype(BF16)) + b_ref[0]


def _ada_all(cs, ada_w, ada_b):
    rows = cs.shape[0]
    tn = 1536
    return pl.pallas_call(
        _ada_body,
        out_shape=jax.ShapeDtypeStruct((DEPTH, rows, 6 * D_MODEL), F32),
        grid=(DEPTH, 6 * D_MODEL // tn),
        in_specs=[
            pl.BlockSpec((rows, D_MODEL), lambda i, n: (0, 0)),
            pl.BlockSpec((1, D_MODEL, tn), lambda i, n: (i, 0, n)),
            pl.BlockSpec((1, 1, tn), lambda i, n: (i, 0, n)),
        ],
        out_specs=pl.BlockSpec((1, rows, tn), lambda i, n: (i, 0, n)),
        compiler_params=_params(40, 2),
        name="ada_mod",
    )(cs, ada_w, ada_b.reshape(DEPTH, 1, 6 * D_MODEL))


def _tok_spec(width, tm=ROW_TILE):
    return pl.BlockSpec((1, tm, width), lambda b, t: (b, t, 0))


def _mod_spec():
    return pl.BlockSpec((1, 6, D_MODEL), lambda b, t: (b, 0, 0))


def _rope_spec(tm=ROW_TILE):
    return pl.BlockSpec((tm, LANES), lambda b, t: (t, 0))


PROJ_TILE = 1024
PROJ_CHAIN = 256
MLA_PROJ_CHAIN = 512


def _chains(ref, chain=PROJ_CHAIN):
    return [slice(r, r + chain) for r in range(0, ref.shape[1], chain)]


def _gqa_proj_body(*refs, rope):
    if rope:
        x_ref, m_ref, w_ref, qn_ref, kn_ref, cos_ref, sin_ref, q_ref, k_ref, v_ref = refs
    else:
        x_ref, m_ref, w_ref, qn_ref, kn_ref, q_ref, k_ref, v_ref = refs
    hd = GQA_HEAD_DIM
    gains = (qn_ref[...] * (hd ** -0.5 * LOG2E), kn_ref[...])
    for rows in _chains(x_ref):
        h = _modulate(x_ref[0, rows, :], m_ref, 0, 1).astype(BF16)
        qkv = _dot(h, w_ref[...])
        if rope:
            cos, sin = cos_ref[rows, :], sin_ref[rows, :]
            mix = [(cos * g, sin * _rot_half(g)) for g in gains]
        for i in range(GQA_Q_HEADS + GQA_KV_HEADS):
            is_q = i < GQA_Q_HEADS
            t = qkv[:, i * hd:(i + 1) * hd]
            t = t * lax.rsqrt(jnp.mean(t * t, -1, keepdims=True) + RMS_EPS)
            if rope:
                a, b = mix[0 if is_q else 1]
                t = t * a + _rot_half(t) * b
            else:
                t = t * gains[0 if is_q else 1]
            if is_q:
                q_ref[0, rows, i * hd:(i + 1) * hd] = t.astype(BF16)
            else:
                j = i - GQA_Q_HEADS
                k_ref[0, rows, j * hd:(j + 1) * hd] = t.astype(BF16)
        v_ref[0, rows, :] = qkv[:, (GQA_Q_HEADS + GQA_KV_HEADS) * hd:].astype(BF16)


def _gqa_proj(x, mod, w, qn, kn, tables):
    bx, sx, _ = x.shape
    rope = tables is not None
    nk = GQA_KV_HEADS * GQA_HEAD_DIM
    tm = min(PROJ_TILE, sx)
    ins = [x, mod, w, qn, kn]
    specs = [_tok_spec(D_MODEL, tm), _mod_spec(), _full_spec(w, 2), _full_spec(qn, 2),
             _full_spec(kn, 2)]
    if rope:
        ins += list(tables)
        specs += [_rope_spec(tm), _rope_spec(tm)]
    return pl.pallas_call(
        functools.partial(_gqa_proj_body, rope=rope),
        out_shape=(
            jax.ShapeDtypeStruct((bx, sx, D_MODEL), BF16),
            jax.ShapeDtypeStruct((bx, sx, nk), BF16),
            jax.ShapeDtypeStruct((bx, sx, nk), BF16),
        ),
        grid=(bx, sx // tm),
        in_specs=specs,
        out_specs=(_tok_spec(D_MODEL, tm), _tok_spec(nk, tm), _tok_spec(nk, tm)),
        compiler_params=_params(48, 2),
        name="gqa_proj",
    )(*ins)


def _mla_proj_body(*refs, rope, need_q):
    refs = list(refs)
    x_ref, m_ref = refs[:2]
    del refs[:2]
    if need_q:
        wdq_ref, qn_ref, wuq_ref = refs[:3]
        del refs[:3]
    wdkv_ref, kvn_ref, wukv_ref = refs[:3]
    del refs[:3]
    if rope:
        cos_ref, sin_ref = refs[:2]
        del refs[:2]
    if need_q:
        q_ref = refs.pop(0)
    k_ref, v_ref = refs

    scale = (MLA_NOPE + MLA_ROPE) ** -0.5 * LOG2E
    pw = MLA_QK_PAD
    for rows in _chains(x_ref, MLA_PROJ_CHAIN):
        h = _modulate(x_ref[0, rows, :], m_ref, 0, 1).astype(BF16)
        if rope:
            cos, sin = cos_ref[rows, :], sin_ref[rows, :]
            cos_q, sin_q = cos * scale, sin * scale

        if need_q:
            cq = _rms(_dot(h, wdq_ref[...]), qn_ref[...]).astype(BF16)
            q = _dot(cq, wuq_ref[...])
            for i in range(MLA_HEADS):
                nope = q[:, i * pw:i * pw + MLA_NOPE]
                pe = q[:, i * pw + MLA_NOPE:(i + 1) * pw]
                pe = pe * cos_q + _rot_half(pe) * sin_q if rope else pe * scale
                q_ref[0, rows, i * pw:i * pw + MLA_NOPE] = (nope * scale).astype(BF16)
                q_ref[0, rows, i * pw + MLA_NOPE:(i + 1) * pw] = pe.astype(BF16)

        ckv = _dot(h, wdkv_ref[...])
        c_kv = _rms(ckv[:, :MLA_KV_RANK], kvn_ref[...]).astype(BF16)
        k_pe = ckv[:, MLA_KV_RANK:]
        if rope:
            k_pe = k_pe * cos + _rot_half(k_pe) * sin
        k_pe = k_pe.astype(BF16)
        kv = _dot(c_kv, wukv_ref[...])
        for i in range(MLA_HEADS):
            k_ref[0, rows, i * pw:i * pw + MLA_NOPE] = (
                kv[:, i * MLA_NOPE:(i + 1) * MLA_NOPE].astype(BF16))
            k_ref[0, rows, i * pw + MLA_NOPE:(i + 1) * pw] = k_pe
        v_ref[0, rows, :] = kv[:, MLA_HEADS * MLA_NOPE:].astype(BF16)


def _mla_proj(x, mod, wq, wkv, tables, need_q):
    bx, sx, _ = x.shape
    rope = tables is not None
    tm = min(PROJ_TILE, sx)
    ins = [x, mod]
    specs = [_tok_spec(D_MODEL, tm), _mod_spec()]
    weights = (list(wq) if need_q else []) + list(wkv)
    ins += weights
    specs += [_full_spec(w, 2) for w in weights]
    if rope:
        ins += list(tables)
        specs += [_rope_spec(tm), _rope_spec(tm)]
    kw = MLA_HEADS * MLA_QK_PAD
    vw = MLA_HEADS * MLA_V
    shapes = [jax.ShapeDtypeStruct((bx, sx, kw), BF16), jax.ShapeDtypeStruct((bx, sx, vw), BF16)]
    ospecs = [_tok_spec(kw, tm), _tok_spec(vw, tm)]
    if need_q:
        shapes.insert(0, jax.ShapeDtypeStruct((bx, sx, kw), BF16))
        ospecs.insert(0, _tok_spec(kw, tm))
    out = pl.pallas_call(
        functools.partial(_mla_proj_body, rope=rope, need_q=need_q),
        out_shape=tuple(shapes),
        grid=(bx, sx // tm),
        in_specs=specs,
        out_specs=tuple(ospecs),
        compiler_params=_params(48, 2),
        name="mla_proj",
    )(*ins)
    return out if need_q else (None,) + tuple(out)


def _na_proj_body(x_ref, m_ref, w_ref, b_ref, q_ref, k_ref, v_ref):
    scale = NA_HEAD_DIM ** -0.5 * LOG2E
    for rows in _chains(x_ref):
        h = _modulate(x_ref[0, rows, :], m_ref, 0, 1).astype(BF16)
        qkv = _dot(h, w_ref[...]) + b_ref[...]
        q_ref[0, rows, :] = (qkv[:, :D_MODEL] * scale).astype(BF16)
        k_ref[0, rows, :] = qkv[:, D_MODEL:2 * D_MODEL].astype(BF16)
        v_ref[0, rows, :] = qkv[:, 2 * D_MODEL:].astype(BF16)


def _na_proj(x, mod, w, b):
    bx, sx, _ = x.shape
    tm = min(PROJ_TILE, sx)
    shape = jax.ShapeDtypeStruct((bx, sx, D_MODEL), BF16)
    return pl.pallas_call(
        _na_proj_body,
        out_shape=(shape, shape, shape),
        grid=(bx, sx // tm),
        in_specs=[_tok_spec(D_MODEL, tm), _mod_spec(), _full_spec(w, 2), _full_spec(b, 2)],
        out_specs=(_tok_spec(D_MODEL, tm),) * 3,
        compiler_params=_params(56, 2),
        name="na_proj",
    )(x, mod, w, b)


LOG2E = 1.4426950408889634


def _augment(v_block, sub, dv):
    col = lax.broadcasted_iota(jnp.int32, v_block.shape, 1)
    if dv == LANES:
        return jnp.concatenate([v_block, jnp.where(col == 0, 1.0, 0.0).astype(BF16)], -1)
    mine = (col < dv) if sub == 0 else (col >= dv)
    ones = jnp.where(col == (dv if sub == 0 else 0), 1.0, 0.0)
    return jnp.where(mine, v_block.astype(F32), ones).astype(BF16)


def _own_lanes(q_block, sub, dk):
    col = lax.broadcasted_iota(jnp.int32, q_block.shape, 1)
    mine = (col >= sub * dk) & (col < (sub + 1) * dk)
    return jnp.where(mine, q_block.astype(F32), 0.0).astype(BF16)


def _normalised(acc, sub, dv):
    ones_col = LANES if dv == LANES else (dv if sub == 0 else 0)
    return acc[:, :LANES] * (1.0 / acc[:, ones_col:ones_col + 1])


def _softmax_pv(q, keys, values, biases):
    scores = [_dot_nt(q, k) if b is None else _dot_nt(q, k) + b for k, b in zip(keys, biases)]
    m = functools.reduce(jnp.maximum, [jnp.max(s, -1, keepdims=True) for s in scores])
    return functools.reduce(
        jnp.add, [_dot(jnp.exp2(s - m).astype(BF16), v) for s, v in zip(scores, values)])


def _store_heads(o_ref, rows, block, outs, dv):
    if dv == LANES:
        for g, o in enumerate(outs):
            c0 = (block * len(outs) + g) * LANES
            o_ref[0, rows, c0:c0 + LANES] = o.astype(o_ref.dtype)
    else:
        col = lax.broadcasted_iota(jnp.int32, outs[0].shape, 1)
        pair = jnp.where(col < dv, outs[0], outs[1])
        o_ref[0, rows, block * LANES:(block + 1) * LANES] = pair.astype(o_ref.dtype)


def _attn_body(*refs, heads, group, dk, dv, has_lat, chain):
    if has_lat:
        q_ref, kc_ref, vc_ref, k_ref, v_ref, o_ref = refs
    else:
        q_ref, kc_ref, vc_ref, o_ref = refs
    tq = q_ref.shape[1]
    per_block = LANES // dv
    assert per_block == 1 or group == 1
    value_refs = [vc_ref, v_ref] if has_lat else [vc_ref]
    key_refs = [kc_ref, k_ref] if has_lat else [kc_ref]
    values = [[_augment(ref[0, :, (i // per_block) * LANES:(i // per_block + 1) * LANES],
                        i % per_block, dv) for ref in value_refs] for i in range(heads)]
    for r in range(tq // chain):
        rows = slice(r * chain, (r + 1) * chain)
        for block in range(heads // per_block):
            outs = []
            for sub in range(per_block):
                i = block * per_block + sub
                keys = [ref[0, :, i * dk:(i + 1) * dk] for ref in key_refs]
                for g in range(group):
                    c0 = (i * group + g) * dk
                    acc = _softmax_pv(q_ref[0, rows, c0:c0 + dk], keys, values[i],
                                      [None] * len(keys))
                    outs.append(_normalised(acc, sub, dv))
            _store_heads(o_ref, rows, block, outs, dv)


def _attention(q, kc, vc, k, v, *, heads, group, dk, dv, tq, chain=256):
    b, sq, qw = q.shape
    has_lat = k is not None
    n_blocks = qw // (heads * group * dk)
    qb, kb, vb = heads * group * dk, heads * dk, heads * dv
    ins = [q, kc, vc]
    specs = [
        pl.BlockSpec((1, tq, qb), lambda bi, hi, ti: (bi, ti, hi)),
        pl.BlockSpec((1, kc.shape[1], kb), lambda bi, hi, ti: (bi, 0, hi)),
        pl.BlockSpec((1, vc.shape[1], vb), lambda bi, hi, ti: (bi, 0, hi)),
    ]
    if has_lat:
        ins += [k, v]
        specs += [
            pl.BlockSpec((1, k.shape[1], kb), lambda bi, hi, ti: (bi, 0, hi)),
            pl.BlockSpec((1, v.shape[1], vb), lambda bi, hi, ti: (bi, 0, hi)),
        ]
    ob = heads * group * dv
    return pl.pallas_call(
        functools.partial(_attn_body, heads=heads, group=group, dk=dk, dv=dv, has_lat=has_lat,
                          chain=min(chain, tq)),
        out_shape=jax.ShapeDtypeStruct((b, sq, n_blocks * ob), BF16),
        grid=(b, n_blocks, sq // tq),
        in_specs=specs,
        out_specs=pl.BlockSpec((1, tq, ob), lambda bi, hi, ti: (bi, ti, hi)),
        compiler_params=_params(48, 3),
        name="attn_lat" if has_lat else "attn_ctx",
    )(*ins)


def _na_window(j):
    return min(max(NA_Q_ROWS * j - NA_WIN_ROWS // 2, 0), N_ROWS - NA_K_ROWS)


def _na_bias_pairs(rpb):
    c = jnp.arange(GRID_W, dtype=jnp.int32)
    col_start = jnp.clip(c - NA_WIN_COLS // 2, 0, GRID_W - NA_WIN_COLS)
    col_ok = (c[None, :] >= col_start[:, None]) & (c[None, :] < col_start[:, None] + NA_WIN_COLS)
    edge = GRID_W - NA_WIN_COLS
    ext = jnp.pad(rpb, ((0, 0), (0, 0), (edge, edge)), mode="edge")
    period = jnp.pad(ext, ((0, 0), (0, 0), (0, 1)))
    n = 2 * GRID_W - 1
    toep = jnp.tile(period, (1, 1, GRID_W))[..., :GRID_W * n]
    slab = toep.reshape(rpb.shape[0], rpb.shape[1], GRID_W, n)[..., GRID_W - 1:]
    slab = jnp.where(col_ok[None, None], slab * LOG2E, -jnp.inf)
    beyond = jnp.full_like(slab[:, :1], -jnp.inf)
    slab = jnp.concatenate([beyond, slab, beyond], 1)
    return jnp.concatenate([slab[:, :-1], slab[:, 1:]], -1)


def _na_bias_pieces(bias_ref, head, j, a, lane):
    w0 = _na_window(j)
    r = NA_Q_ROWS * j + a
    rs = min(max(r - NA_WIN_ROWS // 2, 0), N_ROWS - NA_WIN_ROWS)
    pieces = []
    for kr in range(w0, w0 + NA_K_ROWS, 2):
        first, second = (rs <= k < rs + NA_WIN_ROWS for k in (kr, kr + 1))
        if not (first or second):
            pieces.append(None)
            continue
        piece = bias_ref[head, kr - r + NA_WIN_ROWS]
        if not second:
            piece = jnp.where(lane < GRID_W, piece, -jnp.inf)
        elif not first:
            piece = jnp.where(lane >= GRID_W, piece, -jnp.inf)
        pieces.append(piece)
    return pieces


def _na_attn_body(q_ref, kc_ref, vc_ref, k_ref, v_ref, bias_ref, o_ref):
    hd = NA_HEAD_DIM
    heads = LANES // hd
    qn = NA_Q_ROWS * GRID_W
    vc_aug = [_augment(vc_ref[0], i, hd) for i in range(heads)]
    v_aug = [_augment(v_ref[0], i, hd) for i in range(heads)]
    lane = lax.broadcasted_iota(jnp.int32, (GRID_W, LANES), 1)
    no_prob = jnp.zeros((GRID_W, LANES), BF16)
    for j in range(N_ROWS // NA_Q_ROWS):
        w0 = _na_window(j)
        krows = slice(w0 * GRID_W, (w0 + NA_K_ROWS) * GRID_W)
        rows = slice(j * qn, (j + 1) * qn)
        outs = []
        for i in range(heads):
            q = _own_lanes(q_ref[0, rows, :], i, hd)
            s_ctx = _dot_nt(q, kc_ref[0])
            s_lat = _dot_nt(q, k_ref[0, krows, :])
            p_ctx, p_lat = [], []
            for a in range(NA_Q_ROWS):
                qs = slice(a * GRID_W, (a + 1) * GRID_W)
                pieces = _na_bias_pieces(bias_ref, i, j, a, lane)
                sc = s_ctx[qs]
                sl = [None if b is None else s_lat[qs, p * LANES:(p + 1) * LANES] + b
                      for p, b in enumerate(pieces)]
                chunks = [sc[:, c0:c0 + LANES] for c0 in range(0, sc.shape[1], LANES)]
                chunks += [s for s in sl if s is not None]
                m = jnp.max(functools.reduce(jnp.maximum, chunks), -1, keepdims=True)
                p_ctx.append(jnp.exp2(sc - m).astype(BF16))
                p_lat.append(jnp.concatenate(
                    [no_prob if s is None else jnp.exp2(s - m).astype(BF16) for s in sl], -1))
            acc = (_dot(jnp.concatenate(p_ctx, 0), vc_aug[i])
                   + _dot(jnp.concatenate(p_lat, 0), v_aug[i][krows]))
            outs.append(_normalised(acc, i, hd))
        _store_heads(o_ref, rows, 0, outs, hd)


def _na_attention(q, kc, vc, k, v, bias):
    b, s, _ = q.shape
    n_pairs = D_MODEL // LANES
    heads = LANES // NA_HEAD_DIM
    lat = pl.BlockSpec((1, s, LANES), lambda hi, bi: (bi, 0, hi))
    ctx = pl.BlockSpec((1, CTX_LEN, LANES), lambda hi, bi: (bi, 0, hi))
    bias_spec = pl.BlockSpec((heads,) + bias.shape[1:], lambda hi, bi: (hi, 0, 0, 0))
    return pl.pallas_call(
        _na_attn_body,
        out_shape=jax.ShapeDtypeStruct((b, s, D_MODEL), BF16),
        grid=(n_pairs, b),
        in_specs=[lat, ctx, ctx, lat, lat, bias_spec],
        out_specs=lat,
        compiler_params=_params(48, 2),
        name="na_attn",
    )(q, kc, vc, k, v, bias)


MLP_CHUNK = 1024
TAIL_CHAIN = 512


def _tail_body(a_ref, wo_ref, x_ref, m_ref, g1_ref, b1_ref, w1_ref, w2_ref, g2_ref, b2_ref, o_ref):
    chains = _chains(x_ref, TAIL_CHAIN)
    x1s, hs = [], []
    for rows in chains:
        y = _dot(a_ref[0, rows, :], wo_ref[...])
        x1 = _layer_norm(ALPHA * x_ref[0, rows, :] + m_ref[0, 2:3, :] * y, g1_ref[...], b1_ref[...])
        x1s.append(x1)
        hs.append(_modulate(x1, m_ref, 3, 4).astype(BF16))
    accs = [None] * len(chains)
    for c in range(D_FF // MLP_CHUNK):
        cols = slice(c * MLP_CHUNK, (c + 1) * MLP_CHUNK)
        for r, h in enumerate(hs):
            u = jnp.square(jnp.maximum(_dot(h, w1_ref[:, cols]), 0.0)).astype(BF16)
            part = _dot(u, w2_ref[cols, :])
            accs[r] = part if accs[r] is None else accs[r] + part
    for rows, x1, acc in zip(chains, x1s, accs):
        z = ALPHA * x1 + m_ref[0, 5:6, :] * acc
        o_ref[0, rows, :] = _layer_norm(z, g2_ref[...], b2_ref[...])


def _layer_tail(a, wo, x, mod, g1, b1, w1, w2, g2, b2):
    bx, sx, _ = x.shape
    small = [_full_spec(g1, 2)] * 2
    tm = min(ROW_TILE, sx)
    return pl.pallas_call(
        _tail_body,
        out_shape=jax.ShapeDtypeStruct(x.shape, F32),
        grid=(bx, sx // tm),
        in_specs=[_tok_spec(a.shape[-1], tm), _full_spec(wo, 2, single=True),
                  _tok_spec(D_MODEL, tm), _mod_spec(), *small, _full_spec(w1, 2, single=True),
                  _full_spec(w2, 2, single=True), *small],
        out_specs=_tok_spec(D_MODEL, tm),
        compiler_params=_params(56, 2),
        name="layer_tail",
    )(a, wo, x, mod, g1, b1, w1, w2, g2, b2)


def _row(v):
    return v.reshape(1, -1)


def kernel(x, c, ctx, c_ctx, ada_w, ada_b, ln1_g, ln1_b, ln2_g, ln2_b, mlp_w1, mlp_w2, mla_w_dq, mla_q_norm, mla_w_uq, mla_w_dkv, mla_kv_norm, mla_w_ukv, mla_w_o, na_w_qkv, na_b_qkv, na_rpb, na_w_o, gqa_w_qkv, gqa_q_norm, gqa_k_norm, gqa_w_o):
    pad_rows = 24 - BATCH - 1
    cs = jnp.concatenate([c, c_ctx[None, :], jnp.zeros((pad_rows, D_MODEL), F32)], 0)
    mod_all = _ada_all(cs, ada_w, ada_b).reshape(DEPTH, 24, 6, D_MODEL)

    xc = ctx.reshape(1, BATCH * CTX_LEN, D_MODEL)
    gqa_tables = _rope_tables(GQA_HEAD_DIM)
    mla_tables = _rope_tables(MLA_ROPE)

    def per_batch(t):
        return t.reshape(BATCH, CTX_LEN, t.shape[-1])

    for i in range(DEPTH):
        kind, j = i % N_MIXERS, i // N_MIXERS
        need_ctx = i < DEPTH - 1
        mod = mod_all[i, :BATCH]
        mod_c = mod_all[i, BATCH:BATCH + 1]

        if kind == 0:
            wuq = mla_w_uq[j].reshape(MLA_Q_RANK, MLA_HEADS, MLA_NOPE + MLA_ROPE)
            wuq = jnp.concatenate(
                [wuq[..., :MLA_NOPE], _split_rotary(wuq[..., MLA_NOPE:], MLA_ROPE)], -1)
            wq = (mla_w_dq[j].astype(BF16), _row(mla_q_norm[j]),
                  wuq.reshape(MLA_Q_RANK, MLA_HEADS * MLA_QK_PAD).astype(BF16))
            wdkv = mla_w_dkv[j]
            wdkv = jnp.concatenate(
                [wdkv[:, :MLA_KV_RANK], _split_rotary(wdkv[:, MLA_KV_RANK:], MLA_ROPE)], -1)
            wukv = mla_w_ukv[j].reshape(MLA_KV_RANK, MLA_HEADS, 2, MLA_NOPE)
            wukv = wukv.transpose(0, 2, 1, 3).reshape(MLA_KV_RANK, 2 * MLA_HEADS * MLA_NOPE)
            wkv = (wdkv.astype(BF16), _row(mla_kv_norm[j]), wukv.astype(BF16))
            q, k, v = _mla_proj(x, mod, wq, wkv, mla_tables, True)
            qc, kc, vc = _mla_proj(xc, mod_c, wq, wkv, None, need_ctx)
            kc, vc = per_batch(kc), per_batch(vc)
            dims = dict(heads=1, group=1, dk=MLA_QK_PAD, dv=MLA_V)
            y = _attention(q, kc, vc, k, v, tq=SEQ, **dims)
            if need_ctx:
                yc = _attention(per_batch(qc), kc, vc, None, None, tq=CTX_LEN,
                                **{**dims, "heads": MLA_HEADS})
            w_o = mla_w_o[j]
        elif kind == 1:
            w, b = na_w_qkv[j].astype(BF16), _row(na_b_qkv[j])
            q, k, v = _na_proj(x, mod, w, b)
            qc, kc, vc = _na_proj(xc, mod_c, w, b)
            qc, kc, vc = per_batch(qc), per_batch(kc), per_batch(vc)
            y = _na_attention(q, kc, vc, k, v, _na_bias_pairs(na_rpb[j]))
            if need_ctx:
                yc = _attention(qc, kc, vc, None, None, tq=CTX_LEN, heads=NA_HEADS,
                                group=1, dk=NA_HEAD_DIM, dv=NA_HEAD_DIM)
            w_o = na_w_o[j]
        else:
            n_rot = (GQA_Q_HEADS + GQA_KV_HEADS) * GQA_HEAD_DIM
            w = gqa_w_qkv[j]
            w_rot = w[:, :n_rot].reshape(D_MODEL, GQA_Q_HEADS + GQA_KV_HEADS, GQA_HEAD_DIM)
            w_rot = _split_rotary(w_rot, GQA_HEAD_DIM).reshape(D_MODEL, n_rot)
            w = jnp.concatenate([w_rot, w[:, n_rot:]], -1).astype(BF16)
            qn = _row(_split_rotary(gqa_q_norm[j], GQA_HEAD_DIM))
            kn = _row(_split_rotary(gqa_k_norm[j], GQA_HEAD_DIM))
            q, k, v = _gqa_proj(x, mod, w, qn, kn, gqa_tables)
            qc, kc, vc = _gqa_proj(xc, mod_c, w, qn, kn, None)
            qc, kc, vc = per_batch(qc), per_batch(kc), per_batch(vc)
            dims = dict(heads=1, group=GQA_GROUP, dk=GQA_HEAD_DIM, dv=GQA_HEAD_DIM)
            y = _attention(q, kc, vc, k, v, tq=512, **dims)
            if need_ctx:
                yc = _attention(qc, kc, vc, None, None, tq=CTX_LEN,
                                **{**dims, "heads": GQA_KV_HEADS})
            w_o = gqa_w_o[j]

        w_o = w_o.astype(BF16)
        w1, w2 = mlp_w1[i].astype(BF16), mlp_w2[i].astype(BF16)
        g1, b1, g2, b2 = _row(ln1_g[i]), _row(ln1_b[i]), _row(ln2_g[i]), _row(ln2_b[i])
        x = _layer_tail(y, w_o, x, mod, g1, b1, w1, w2, g2, b2)
        if need_ctx:
            yc = yc.reshape(1, BATCH * CTX_LEN, D_MODEL)
            xc = _layer_tail(yc, w_o, xc, mod_c, g1, b1, w1, w2, g2, b2)
    return x
```

```python
import functools

import jax
import jax.numpy as jnp
from jax import lax
from jax.experimental import pallas as pl
from jax.experimental.pallas import tpu as pltpu

D_MODEL = 1024
BATCH = 16
SEQ = 2048
DEPTH = 4
GRID_W = 64
CTX_LEN = 256
N_MIXERS = 3
ROPE_THETA = 10000.0
LN_EPS = 1e-5
RMS_EPS = 1e-6
D_FF = 4 * D_MODEL

MLA_HEADS = 8
MLA_Q_RANK = 512
MLA_KV_RANK = 256
MLA_NOPE = 128
MLA_ROPE = 64
MLA_V = 128
MLA_QK_PAD = 256

NA_HEADS = 16
NA_HEAD_DIM = D_MODEL // NA_HEADS
NA_WIN_ROWS = 8
NA_WIN_COLS = 16
NA_Q_ROWS = 4
NA_K_ROWS = 12
N_ROWS = SEQ // GRID_W

GQA_HEAD_DIM = 128
GQA_Q_HEADS = D_MODEL // GQA_HEAD_DIM
GQA_KV_HEADS = GQA_Q_HEADS // 4
GQA_GROUP = GQA_Q_HEADS // GQA_KV_HEADS

ALPHA = (2.0 * DEPTH) ** 0.25
LANES = 128
ROW_TILE = 1024
F32 = jnp.float32
BF16 = jnp.bfloat16
MIB = 1024 * 1024


def _params(vmem_mib, n_axes):
    return pltpu.CompilerParams(
        dimension_semantics=("arbitrary",) * n_axes, vmem_limit_bytes=vmem_mib * MIB
    )


def _full_spec(arr, n_axes, single=False):
    zeros = (0,) * arr.ndim
    index_map = {2: lambda a, b: zeros, 3: lambda a, b, c: zeros}[n_axes]
    if single:
        return pl.BlockSpec(arr.shape, index_map, pipeline_mode=pl.Buffered(1))
    return pl.BlockSpec(arr.shape, index_map)


def _dot(a, b):
    return jnp.dot(a, b, preferred_element_type=F32)


def _dot_nt(a, b):
    return lax.dot_general(a, b, (((1,), (1,)), ((), ())), preferred_element_type=F32)


def _modulate(x, m_ref, shift_row, scale_row):
    shift = m_ref[0, shift_row:shift_row + 1, :]
    scale = m_ref[0, scale_row:scale_row + 1, :]
    return x * (1.0 + scale) + shift


def _rms(t, g):
    return t * lax.rsqrt(jnp.mean(t * t, -1, keepdims=True) + RMS_EPS) * g


def _layer_norm(z, g, b):
    mu = jnp.mean(z, -1, keepdims=True)
    zc = z - mu
    var = jnp.mean(zc * zc, -1, keepdims=True)
    return zc * lax.rsqrt(var + LN_EPS) * g + b


HALF_LANES = LANES // 2


def _rot_half(t):
    return pltpu.roll(t, HALF_LANES, 1)


def _split_rotary(t, d):
    lead = t.shape[:-1]
    t = t.reshape(lead + (2, 2, d // 4)).swapaxes(-3, -2).reshape(lead + (2, d // 2))
    t = jnp.pad(t, [(0, 0)] * (len(lead) + 1) + [(0, HALF_LANES - d // 2)])
    return t.reshape(lead + (LANES,))


def _rope_tables(d):
    t = jnp.arange(SEQ, dtype=jnp.int32)
    rows, cols = t // GRID_W, t % GRID_W
    quarter = d // 4
    freqs = ROPE_THETA ** (-jnp.arange(quarter, dtype=F32) / quarter)
    ang = jnp.concatenate([rows.astype(F32)[:, None] * freqs, cols.astype(F32)[:, None] * freqs], -1)
    c, s = jnp.cos(ang), jnp.sin(ang)
    pad = jnp.zeros((SEQ, HALF_LANES - d // 2), F32)
    return jnp.concatenate([c, pad, c, pad], -1), jnp.concatenate([-s, pad, s, pad], -1)


def _ada_body(c_ref, w_ref, b_ref, o_ref):
    c = c_ref[...]
    s = (c * (1.0 / (1.0 + jnp.exp(-c)))).astype(BF16)
    o_ref[0] = _dot(s, w_ref[0].astype(BF16)) + b_ref[0]


def _ada_all(cs, ada_w, ada_b):
    rows = cs.shape[0]
    tn = 1536
    return pl.pallas_call(
        _ada_body,
        out_shape=jax.ShapeDtypeStruct((DEPTH, rows, 6 * D_MODEL), F32),
        grid=(DEPTH, 6 * D_MODEL // tn),
        in_specs=[
            pl.BlockSpec((rows, D_MODEL), lambda i, n: (0, 0)),
            pl.BlockSpec((1, D_MODEL, tn), lambda i, n: (i, 0, n)),
            pl.BlockSpec((1, 1, tn), lambda i, n: (i, 0, n)),
        ],
        out_specs=pl.BlockSpec((1, rows, tn), lambda i, n: (i, 0, n)),
        compiler_params=_params(40, 2),
        name="ada_mod",
    )(cs, ada_w, ada_b.reshape(DEPTH, 1, 6 * D_MODEL))


def _tok_spec(width, tm=ROW_TILE):
    return pl.BlockSpec((1, tm, width), lambda b, t: (b, t, 0))


def _mod_spec():
    return pl.BlockSpec((1, 6, D_MODEL), lambda b, t: (b, 0, 0))


def _rope_spec(tm=ROW_TILE):
    return pl.BlockSpec((tm, LANES), lambda b, t: (t, 0))


PROJ_TILE = 1024
PROJ_CHAIN = 256
MLA_PROJ_CHAIN = 512


def _chains(ref, chain=PROJ_CHAIN):
    return [slice(r, r + chain) for r in range(0, ref.shape[1], chain)]


def _gqa_proj_body(*refs, rope):
    if rope:
        x_ref, m_ref, w_ref, qn_ref, kn_ref, cos_ref, sin_ref, q_ref, k_ref, v_ref = refs
    else:
        x_ref, m_ref, w_ref, qn_ref, kn_ref, q_ref, k_ref, v_ref = refs
    hd = GQA_HEAD_DIM
    gains = (qn_ref[...] * (hd ** -0.5 * LOG2E), kn_ref[...])
    for rows in _chains(x_ref):
        h = _modulate(x_ref[0, rows, :], m_ref, 0, 1).astype(BF16)
        qkv = _dot(h, w_ref[...])
        if rope:
            cos, sin = cos_ref[rows, :], sin_ref[rows, :]
            mix = [(cos * g, sin * _rot_half(g)) for g in gains]
        for i in range(GQA_Q_HEADS + GQA_KV_HEADS):
            is_q = i < GQA_Q_HEADS
            t = qkv[:, i * hd:(i + 1) * hd]
            t = t * lax.rsqrt(jnp.mean(t * t, -1, keepdims=True) + RMS_EPS)
            if rope:
                a, b = mix[0 if is_q else 1]
                t = t * a + _rot_half(t) * b
            else:
                t = t * gains[0 if is_q else 1]
            if is_q:
                q_ref[0, rows, i * hd:(i + 1) * hd] = t.astype(BF16)
            else:
                j = i - GQA_Q_HEADS
                k_ref[0, rows, j * hd:(j + 1) * hd] = t.astype(BF16)
        v_ref[0, rows, :] = qkv[:, (GQA_Q_HEADS + GQA_KV_HEADS) * hd:].astype(BF16)


def _gqa_proj(x, mod, w, qn, kn, tables):
    bx, sx, _ = x.shape
    rope = tables is not None
    nk = GQA_KV_HEADS * GQA_HEAD_DIM
    tm = min(PROJ_TILE, sx)
    ins = [x, mod, w, qn, kn]
    specs = [_tok_spec(D_MODEL, tm), _mod_spec(), _full_spec(w, 2), _full_spec(qn, 2),
             _full_spec(kn, 2)]
    if rope:
        ins += list(tables)
        specs += [_rope_spec(tm), _rope_spec(tm)]
    return pl.pallas_call(
        functools.partial(_gqa_proj_body, rope=rope),
        out_shape=(
            jax.ShapeDtypeStruct((bx, sx, D_MODEL), BF16),
            jax.ShapeDtypeStruct((bx, sx, nk), BF16),
            jax.ShapeDtypeStruct((bx, sx, nk), BF16),
        ),
        grid=(bx, sx // tm),
        in_specs=specs,
        out_specs=(_tok_spec(D_MODEL, tm), _tok_spec(nk, tm), _tok_spec(nk, tm)),
        compiler_params=_params(48, 2),
        name="gqa_proj",
    )(*ins)


def _mla_proj_body(*refs, rope, need_q):
    refs = list(refs)
    x_ref, m_ref = refs[:2]
    del refs[:2]
    if need_q:
        wdq_ref, qn_ref, wuq_ref = refs[:3]
        del refs[:3]
    wdkv_ref, kvn_ref, wukv_ref = refs[:3]
    del refs[:3]
    if rope:
        cos_ref, sin_ref = refs[:2]
        del refs[:2]
    if need_q:
        q_ref = refs.pop(0)
    k_ref, v_ref = refs

    scale = (MLA_NOPE + MLA_ROPE) ** -0.5 * LOG2E
    pw = MLA_QK_PAD
    for rows in _chains(x_ref, MLA_PROJ_CHAIN):
        h = _modulate(x_ref[0, rows, :], m_ref, 0, 1).astype(BF16)
        if rope:
            cos, sin = cos_ref[rows, :], sin_ref[rows, :]
            cos_q, sin_q = cos * scale, sin * scale

        if need_q:
            cq = _rms(_dot(h, wdq_ref[...]), qn_ref[...]).astype(BF16)
            q = _dot(cq, wuq_ref[...])
            for i in range(MLA_HEADS):
                nope = q[:, i * pw:i * pw + MLA_NOPE]
                pe = q[:, i * pw + MLA_NOPE:(i + 1) * pw]
                pe = pe * cos_q + _rot_half(pe) * sin_q if rope else pe * scale
                q_ref[0, rows, i * pw:i * pw + MLA_NOPE] = (nope * scale).astype(BF16)
                q_ref[0, rows, i * pw + MLA_NOPE:(i + 1) * pw] = pe.astype(BF16)

        ckv = _dot(h, wdkv_ref[...])
        c_kv = _rms(ckv[:, :MLA_KV_RANK], kvn_ref[...]).astype(BF16)
        k_pe = ckv[:, MLA_KV_RANK:]
        if rope:
            k_pe = k_pe * cos + _rot_half(k_pe) * sin
        k_pe = k_pe.astype(BF16)
        kv = _dot(c_kv, wukv_ref[...])
        for i in range(MLA_HEADS):
            k_ref[0, rows, i * pw:i * pw + MLA_NOPE] = (
                kv[:, i * MLA_NOPE:(i + 1) * MLA_NOPE].astype(BF16))
            k_ref[0, rows, i * pw + MLA_NOPE:(i + 1) * pw] = k_pe
        v_ref[0, rows, :] = kv[:, MLA_HEADS * MLA_NOPE:].astype(BF16)


def _mla_proj(x, mod, wq, wkv, tables, need_q):
    bx, sx, _ = x.shape
    rope = tables is not None
    tm = min(PROJ_TILE, sx)
    ins = [x, mod]
    specs = [_tok_spec(D_MODEL, tm), _mod_spec()]
    weights = (list(wq) if need_q else []) + list(wkv)
    ins += weights
    specs += [_full_spec(w, 2) for w in weights]
    if rope:
        ins += list(tables)
        specs += [_rope_spec(tm), _rope_spec(tm)]
    kw = MLA_HEADS * MLA_QK_PAD
    vw = MLA_HEADS * MLA_V
    shapes = [jax.ShapeDtypeStruct((bx, sx, kw), BF16), jax.ShapeDtypeStruct((bx, sx, vw), BF16)]
    ospecs = [_tok_spec(kw, tm), _tok_spec(vw, tm)]
    if need_q:
        shapes.insert(0, jax.ShapeDtypeStruct((bx, sx, kw), BF16))
        ospecs.insert(0, _tok_spec(kw, tm))
    out = pl.pallas_call(
        functools.partial(_mla_proj_body, rope=rope, need_q=need_q),
        out_shape=tuple(shapes),
        grid=(bx, sx // tm),
        in_specs=specs,
        out_specs=tuple(ospecs),
        compiler_params=_params(48, 2),
        name="mla_proj",
    )(*ins)
    return out if need_q else (None,) + tuple(out)


def _na_proj_body(x_ref, m_ref, w_ref, b_ref, q_ref, k_ref, v_ref):
    scale = NA_HEAD_DIM ** -0.5 * LOG2E
    for rows in _chains(x_ref):
        h = _modulate(x_ref[0, rows, :], m_ref, 0, 1).astype(BF16)
        qkv = _dot(h, w_ref[...]) + b_ref[...]
        q_ref[0, rows, :] = (qkv[:, :D_MODEL] * scale).astype(BF16)
        k_ref[0, rows, :] = qkv[:, D_MODEL:2 * D_MODEL].astype(BF16)
        v_ref[0, rows, :] = qkv[:, 2 * D_MODEL:].astype(BF16)


def _na_proj(x, mod, w, b):
    bx, sx, _ = x.shape
    tm = min(PROJ_TILE, sx)
    shape = jax.ShapeDtypeStruct((bx, sx, D_MODEL), BF16)
    return pl.pallas_call(
        _na_proj_body,
        out_shape=(shape, shape, shape),
        grid=(bx, sx // tm),
        in_specs=[_tok_spec(D_MODEL, tm), _mod_spec(), _full_spec(w, 2), _full_spec(b, 2)],
        out_specs=(_tok_spec(D_MODEL, tm),) * 3,
        compiler_params=_params(56, 2),
        name="na_proj",
    )(x, mod, w, b)


LOG2E = 1.4426950408889634


def _augment(v_block, sub, dv):
    col = lax.broadcasted_iota(jnp.int32, v_block.shape, 1)
    if dv == LANES:
        return jnp.concatenate([v_block, jnp.where(col == 0, 1.0, 0.0).astype(BF16)], -1)
    mine = (col < dv) if sub == 0 else (col >= dv)
    ones = jnp.where(col == (dv if sub == 0 else 0), 1.0, 0.0)
    return jnp.where(mine, v_block.astype(F32), ones).astype(BF16)


def _own_lanes(q_block, sub, dk):
    col = lax.broadcasted_iota(jnp.int32, q_block.shape, 1)
    mine = (col >= sub * dk) & (col < (sub + 1) * dk)
    return jnp.where(mine, q_block.astype(F32), 0.0).astype(BF16)


def _normalised(acc, sub, dv):
    ones_col = LANES if dv == LANES else (dv if sub == 0 else 0)
    return acc[:, :LANES] * (1.0 / acc[:, ones_col:ones_col + 1])


def _softmax_pv(q, keys, values, biases):
    scores = [_dot_nt(q, k) if b is None else _dot_nt(q, k) + b for k, b in zip(keys, biases)]
    m = functools.reduce(jnp.maximum, [jnp.max(s, -1, keepdims=True) for s in scores])
    return functools.reduce(
        jnp.add, [_dot(jnp.exp2(s - m).astype(BF16), v) for s, v in zip(scores, values)])


def _store_heads(o_ref, rows, block, outs, dv):
    if dv == LANES:
        for g, o in enumerate(outs):
            c0 = (block * len(outs) + g) * LANES
            o_ref[0, rows, c0:c0 + LANES] = o.astype(o_ref.dtype)
    else:
        col = lax.broadcasted_iota(jnp.int32, outs[0].shape, 1)
        pair = jnp.where(col < dv, outs[0], outs[1])
        o_ref[0, rows, block * LANES:(block + 1) * LANES] = pair.astype(o_ref.dtype)


def _attn_body(*refs, heads, group, dk, dv, has_lat, chain):
    if has_lat:
        q_ref, kc_ref, vc_ref, k_ref, v_ref, o_ref = refs
    else:
        q_ref, kc_ref, vc_ref, o_ref = refs
    tq = q_ref.shape[1]
    per_block = LANES // dv
    assert per_block == 1 or group == 1
    value_refs = [vc_ref, v_ref] if has_lat else [vc_ref]
    key_refs = [kc_ref, k_ref] if has_lat else [kc_ref]
    values = [[_augment(ref[0, :, (i // per_block) * LANES:(i // per_block + 1) * LANES],
                        i % per_block, dv) for ref in value_refs] for i in range(heads)]
    for r in range(tq // chain):
        rows = slice(r * chain, (r + 1) * chain)
        for block in range(heads // per_block):
            outs = []
            for sub in range(per_block):
                i = block * per_block + sub
                keys = [ref[0, :, i * dk:(i + 1) * dk] for ref in key_refs]
                for g in range(group):
                    c0 = (i * group + g) * dk
                    acc = _softmax_pv(q_ref[0, rows, c0:c0 + dk], keys, values[i],
                                      [None] * len(keys))
                    outs.append(_normalised(acc, sub, dv))
            _store_heads(o_ref, rows, block, outs, dv)


def _attention(q, kc, vc, k, v, *, heads, group, dk, dv, tq, chain=256):
    b, sq, qw = q.shape
    has_lat = k is not None
    n_blocks = qw // (heads * group * dk)
    qb, kb, vb = heads * group * dk, heads * dk, heads * dv
    ins = [q, kc, vc]
    specs = [
        pl.BlockSpec((1, tq, qb), lambda bi, hi, ti: (bi, ti, hi)),
        pl.BlockSpec((1, kc.shape[1], kb), lambda bi, hi, ti: (bi, 0, hi)),
        pl.BlockSpec((1, vc.shape[1], vb), lambda bi, hi, ti: (bi, 0, hi)),
    ]
    if has_lat:
        ins += [k, v]
        specs += [
            pl.BlockSpec((1, k.shape[1], kb), lambda bi, hi, ti: (bi, 0, hi)),
            pl.BlockSpec((1, v.shape[1], vb), lambda bi, hi, ti: (bi, 0, hi)),
        ]
    ob = heads * group * dv
    return pl.pallas_call(
        functools.partial(_attn_body, heads=heads, group=group, dk=dk, dv=dv, has_lat=has_lat,
                          chain=min(chain, tq)),
        out_shape=jax.ShapeDtypeStruct((b, sq, n_blocks * ob), BF16),
        grid=(b, n_blocks, sq // tq),
        in_specs=specs,
        out_specs=pl.BlockSpec((1, tq, ob), lambda bi, hi, ti: (bi, ti, hi)),
        compiler_params=_params(48, 3),
        name="attn_lat" if has_lat else "attn_ctx",
    )(*ins)


def _na_window(j):
    return min(max(NA_Q_ROWS * j - NA_WIN_ROWS // 2, 0), N_ROWS - NA_K_ROWS)


def _na_bias_pairs(rpb):
    c = jnp.arange(GRID_W, dtype=jnp.int32)
    col_start = jnp.clip(c - NA_WIN_COLS // 2, 0, GRID_W - NA_WIN_COLS)
    col_ok = (c[None, :] >= col_start[:, None]) & (c[None, :] < col_start[:, None] + NA_WIN_COLS)
    edge = GRID_W - NA_WIN_COLS
    ext = jnp.pad(rpb, ((0, 0), (0, 0), (edge, edge)), mode="edge")
    period = jnp.pad(ext, ((0, 0), (0, 0), (0, 1)))
    n = 2 * GRID_W - 1
    toep = jnp.tile(period, (1, 1, GRID_W))[..., :GRID_W * n]
    slab = toep.reshape(rpb.shape[0], rpb.shape[1], GRID_W, n)[..., GRID_W - 1:]
    slab = jnp.where(col_ok[None, None], slab * LOG2E, -jnp.inf)
    beyond = jnp.full_like(slab[:, :1], -jnp.inf)
    slab = jnp.concatenate([beyond, slab, beyond], 1)
    return jnp.concatenate([slab[:, :-1], slab[:, 1:]], -1)


def _na_bias_pieces(bias_ref, head, j, a, lane):
    w0 = _na_window(j)
    r = NA_Q_ROWS * j + a
    rs = min(max(r - NA_WIN_ROWS // 2, 0), N_ROWS - NA_WIN_ROWS)
    pieces = []
    for kr in range(w0, w0 + NA_K_ROWS, 2):
        first, second = (rs <= k < rs + NA_WIN_ROWS for k in (kr, kr + 1))
        if not (first or second):
            pieces.append(None)
            continue
        piece = bias_ref[head, kr - r + NA_WIN_ROWS]
        if not second:
            piece = jnp.where(lane < GRID_W, piece, -jnp.inf)
        elif not first:
            piece = jnp.where(lane >= GRID_W, piece, -jnp.inf)
        pieces.append(piece)
    return pieces


def _na_attn_body(q_ref, kc_ref, vc_ref, k_ref, v_ref, bias_ref, o_ref):
    hd = NA_HEAD_DIM
    heads = LANES // hd
    qn = NA_Q_ROWS * GRID_W
    vc_aug = _augment(vc_ref[0], 0, LANES)
    v_aug = _augment(v_ref[0], 0, LANES)
    lane = lax.broadcasted_iota(jnp.int32, (GRID_W, LANES), 1)
    no_prob = jnp.zeros((GRID_W, LANES), BF16)
    for j in range(N_ROWS // NA_Q_ROWS):
        w0 = _na_window(j)
        krows = slice(w0 * GRID_W, (w0 + NA_K_ROWS) * GRID_W)
        rows = slice(j * qn, (j + 1) * qn)
        q = jnp.concatenate([_own_lanes(q_ref[0, rows, :], i, hd) for i in range(heads)], 0)
        s_ctx = _dot_nt(q, kc_ref[0])
        s_lat = _dot_nt(q, k_ref[0, krows, :])
        p_ctx, p_lat = [], []
        for i in range(heads):
            for a in range(NA_Q_ROWS):
                qs = slice(i * qn + a * GRID_W, i * qn + (a + 1) * GRID_W)
                pieces = _na_bias_pieces(bias_ref, i, j, a, lane)
                sc = s_ctx[qs]
                sl = [None if b is None else s_lat[qs, p * LANES:(p + 1) * LANES] + b
                      for p, b in enumerate(pieces)]
                chunks = [sc[:, c0:c0 + LANES] for c0 in range(0, sc.shape[1], LANES)]
                chunks += [s for s in sl if s is not None]
                m = jnp.max(functools.reduce(jnp.maximum, chunks), -1, keepdims=True)
                p_ctx.append(jnp.exp2(sc - m).astype(BF16))
                p_lat.append(jnp.concatenate(
                    [no_prob if s is None else jnp.exp2(s - m).astype(BF16) for s in sl], -1))
        acc = (_dot(jnp.concatenate(p_ctx, 0), vc_aug)
               + _dot(jnp.concatenate(p_lat, 0), v_aug[krows]))
        outs = [_normalised(acc[i * qn:(i + 1) * qn], 0, LANES) for i in range(heads)]
        _store_heads(o_ref, rows, 0, outs, hd)


def _na_attention(q, kc, vc, k, v, bias):
    b, s, _ = q.shape
    n_pairs = D_MODEL // LANES
    heads = LANES // NA_HEAD_DIM
    lat = pl.BlockSpec((1, s, LANES), lambda hi, bi: (bi, 0, hi))
    ctx = pl.BlockSpec((1, CTX_LEN, LANES), lambda hi, bi: (bi, 0, hi))
    bias_spec = pl.BlockSpec((heads,) + bias.shape[1:], lambda hi, bi: (hi, 0, 0, 0))
    return pl.pallas_call(
        _na_attn_body,
        out_shape=jax.ShapeDtypeStruct((b, s, D_MODEL), BF16),
        grid=(n_pairs, b),
        in_specs=[lat, ctx, ctx, lat, lat, bias_spec],
        out_specs=lat,
        compiler_params=_params(48, 2),
        name="na_attn",
    )(q, kc, vc, k, v, bias)


MLP_CHUNK = 1024
TAIL_CHAIN = 512


def _tail_body(a_ref, wo_ref, x_ref, m_ref, g1_ref, b1_ref, w1_ref, w2_ref, g2_ref, b2_ref, o_ref):
    chains = _chains(x_ref, TAIL_CHAIN)
    x1s, hs = [], []
    for rows in chains:
        y = _dot(a_ref[0, rows, :], wo_ref[...])
        x1 = _layer_norm(ALPHA * x_ref[0, rows, :] + m_ref[0, 2:3, :] * y, g1_ref[...], b1_ref[...])
        x1s.append(x1)
        hs.append(_modulate(x1, m_ref, 3, 4).astype(BF16))
    accs = [None] * len(chains)
    for c in range(D_FF // MLP_CHUNK):
        cols = slice(c * MLP_CHUNK, (c + 1) * MLP_CHUNK)
        for r, h in enumerate(hs):
            u = jnp.square(jnp.maximum(_dot(h, w1_ref[:, cols]), 0.0)).astype(BF16)
            part = _dot(u, w2_ref[cols, :])
            accs[r] = part if accs[r] is None else accs[r] + part
    for rows, x1, acc in zip(chains, x1s, accs):
        z = ALPHA * x1 + m_ref[0, 5:6, :] * acc
        o_ref[0, rows, :] = _layer_norm(z, g2_ref[...], b2_ref[...])


def _layer_spec(stack, layer):
    return pl.BlockSpec((None,) + stack.shape[1:], lambda b, t: (layer, 0, 0),
                        pipeline_mode=pl.Buffered(1))


def _layer_tail(a, wo, x, mod, g1, b1, w1, w2, g2, b2, layer):
    bx, sx, _ = x.shape
    small = [_full_spec(g1, 2)] * 2
    tm = min(ROW_TILE, sx)
    return pl.pallas_call(
        _tail_body,
        out_shape=jax.ShapeDtypeStruct(x.shape, F32),
        grid=(bx, sx // tm),
        in_specs=[_tok_spec(a.shape[-1], tm), _full_spec(wo, 2, single=True),
                  _tok_spec(D_MODEL, tm), _mod_spec(), *small, _layer_spec(w1, layer),
                  _layer_spec(w2, layer), *small],
        out_specs=_tok_spec(D_MODEL, tm),
        compiler_params=_params(56, 2),
        name="layer_tail",
    )(a, wo, x, mod, g1, b1, w1, w2, g2, b2)


def _row(v):
    return v.reshape(1, -1)


def kernel(x, c, ctx, c_ctx, ada_w, ada_b, ln1_g, ln1_b, ln2_g, ln2_b, mlp_w1, mlp_w2, mla_w_dq, mla_q_norm, mla_w_uq, mla_w_dkv, mla_kv_norm, mla_w_ukv, mla_w_o, na_w_qkv, na_b_qkv, na_rpb, na_w_o, gqa_w_qkv, gqa_q_norm, gqa_k_norm, gqa_w_o):
    pad_rows = 24 - BATCH - 1
    cs = jnp.concatenate([c, c_ctx[None, :], jnp.zeros((pad_rows, D_MODEL), F32)], 0)
    mod_all = _ada_all(cs, ada_w, ada_b).reshape(DEPTH, 24, 6, D_MODEL)

    xc = ctx.reshape(1, BATCH * CTX_LEN, D_MODEL)
    gqa_tables = _rope_tables(GQA_HEAD_DIM)
    mla_tables = _rope_tables(MLA_ROPE)
    w1_all, w2_all = mlp_w1.astype(BF16), mlp_w2.astype(BF16)

    def per_batch(t):
        return t.reshape(BATCH, CTX_LEN, t.shape[-1])

    for i in range(DEPTH):
        kind, j = i % N_MIXERS, i // N_MIXERS
        need_ctx = i < DEPTH - 1
        mod = mod_all[i, :BATCH]
        mod_c = mod_all[i, BATCH:BATCH + 1]

        if kind == 0:
            wuq = mla_w_uq[j].reshape(MLA_Q_RANK, MLA_HEADS, MLA_NOPE + MLA_ROPE)
            wuq = jnp.concatenate(
                [wuq[..., :MLA_NOPE], _split_rotary(wuq[..., MLA_NOPE:], MLA_ROPE)], -1)
            wq = (mla_w_dq[j].astype(BF16), _row(mla_q_norm[j]),
                  wuq.reshape(MLA_Q_RANK, MLA_HEADS * MLA_QK_PAD).astype(BF16))
            wdkv = mla_w_dkv[j]
            wdkv = jnp.concatenate(
                [wdkv[:, :MLA_KV_RANK], _split_rotary(wdkv[:, MLA_KV_RANK:], MLA_ROPE)], -1)
            wukv = mla_w_ukv[j].reshape(MLA_KV_RANK, MLA_HEADS, 2, MLA_NOPE)
            wukv = wukv.transpose(0, 2, 1, 3).reshape(MLA_KV_RANK, 2 * MLA_HEADS * MLA_NOPE)
            wkv = (wdkv.astype(BF16), _row(mla_kv_norm[j]), wukv.astype(BF16))
            q, k, v = _mla_proj(x, mod, wq, wkv, mla_tables, True)
            qc, kc, vc = _mla_proj(xc, mod_c, wq, wkv, None, need_ctx)
            kc, vc = per_batch(kc), per_batch(vc)
            dims = dict(heads=1, group=1, dk=MLA_QK_PAD, dv=MLA_V)
            y = _attention(q, kc, vc, k, v, tq=SEQ, **dims)
            if need_ctx:
                yc = _attention(per_batch(qc), kc, vc, None, None, tq=CTX_LEN,
                                **{**dims, "heads": MLA_HEADS})
            w_o = mla_w_o[j]
        elif kind == 1:
            w, b = na_w_qkv[j].astype(BF16), _row(na_b_qkv[j])
            q, k, v = _na_proj(x, mod, w, b)
            qc, kc, vc = _na_proj(xc, mod_c, w, b)
            qc, kc, vc = per_batch(qc), per_batch(kc), per_batch(vc)
            y = _na_attention(q, kc, vc, k, v, _na_bias_pairs(na_rpb[j]))
            if need_ctx:
                yc = _attention(qc, kc, vc, None, None, tq=CTX_LEN, heads=NA_HEADS,
                                group=1, dk=NA_HEAD_DIM, dv=NA_HEAD_DIM)
            w_o = na_w_o[j]
        else:
            n_rot = (GQA_Q_HEADS + GQA_KV_HEADS) * GQA_HEAD_DIM
            w = gqa_w_qkv[j]
            w_rot = w[:, :n_rot].reshape(D_MODEL, GQA_Q_HEADS + GQA_KV_HEADS, GQA_HEAD_DIM)
            w_rot = _split_rotary(w_rot, GQA_HEAD_DIM).reshape(D_MODEL, n_rot)
            w = jnp.concatenate([w_rot, w[:, n_rot:]], -1).astype(BF16)
            qn = _row(_split_rotary(gqa_q_norm[j], GQA_HEAD_DIM))
            kn = _row(_split_rotary(gqa_k_norm[j], GQA_HEAD_DIM))
            q, k, v = _gqa_proj(x, mod, w, qn, kn, gqa_tables)
            qc, kc, vc = _gqa_proj(xc, mod_c, w, qn, kn, None)
            qc, kc, vc = per_batch(qc), per_batch(kc), per_batch(vc)
            dims = dict(heads=1, group=GQA_GROUP, dk=GQA_HEAD_DIM, dv=GQA_HEAD_DIM)
            y = _attention(q, kc, vc, k, v, tq=512, **dims)
            if need_ctx:
                yc = _attention(qc, kc, vc, None, None, tq=CTX_LEN,
                                **{**dims, "heads": GQA_KV_HEADS})
            w_o = gqa_w_o[j]

        w_o = w_o.astype(BF16)
        g1, b1, g2, b2 = _row(ln1_g[i]), _row(ln1_b[i]), _row(ln2_g[i]), _row(ln2_b[i])
        x = _layer_tail(y, w_o, x, mod, g1, b1, w1_all, w2_all, g2, b2, i)
        if need_ctx:
            yc = yc.reshape(1, BATCH * CTX_LEN, D_MODEL)
            xc = _layer_tail(yc, w_o, xc, mod_c, g1, b1, w1_all, w2_all, g2, b2, i)
    return x
```

```python
import functools

import jax
import jax.numpy as jnp
from jax import lax
from jax.experimental import pallas as pl
from jax.experimental.pallas import tpu as pltpu

D_MODEL = 1024
BATCH = 16
SEQ = 2048
DEPTH = 4
GRID_W = 64
CTX_LEN = 256
N_MIXERS = 3
ROPE_THETA = 10000.0
LN_EPS = 1e-5
RMS_EPS = 1e-6
D_FF = 4 * D_MODEL

MLA_HEADS = 8
MLA_Q_RANK = 512
MLA_KV_RANK = 256
MLA_NOPE = 128
MLA_ROPE = 64
MLA_V = 128
MLA_QK_PAD = 256

NA_HEADS = 16
NA_HEAD_DIM = D_MODEL // NA_HEADS
NA_WIN_ROWS = 8
NA_WIN_COLS = 16
NA_Q_ROWS = 4
NA_K_ROWS = 12
N_ROWS = SEQ // GRID_W

GQA_HEAD_DIM = 128
GQA_Q_HEADS = D_MODEL // GQA_HEAD_DIM
GQA_KV_HEADS = GQA_Q_HEADS // 4
GQA_GROUP = GQA_Q_HEADS // GQA_KV_HEADS

ALPHA = (2.0 * DEPTH) ** 0.25
LANES = 128
ROW_TILE = 1024
F32 = jnp.float32
BF16 = jnp.bfloat16
MIB = 1024 * 1024


def _params(vmem_mib, n_axes):
    return pltpu.CompilerParams(
        dimension_semantics=("arbitrary",) * n_axes, vmem_limit_bytes=vmem_mib * MIB
    )


def _full_spec(arr, n_axes, single=False):
    zeros = (0,) * arr.ndim
    index_map = {2: lambda a, b: zeros, 3: lambda a, b, c: zeros}[n_axes]
    if single:
        return pl.BlockSpec(arr.shape, index_map, pipeline_mode=pl.Buffered(1))
    return pl.BlockSpec(arr.shape, index_map)


def _dot(a, b):
    return jnp.dot(a, b, preferred_element_type=F32)


def _dot_nt(a, b):
    return lax.dot_general(a, b, (((1,), (1,)), ((), ())), preferred_element_type=F32)


def _modulate(x, m_ref, shift_row, scale_row):
    shift = m_ref[0, shift_row:shift_row + 1, :]
    scale = m_ref[0, scale_row:scale_row + 1, :]
    return x * (1.0 + scale) + shift


def _rms(t, g):
    return t * lax.rsqrt(jnp.mean(t * t, -1, keepdims=True) + RMS_EPS) * g


def _layer_norm(z, g, b):
    mu = jnp.mean(z, -1, keepdims=True)
    zc = z - mu
    var = jnp.mean(zc * zc, -1, keepdims=True)
    return zc * lax.rsqrt(var + LN_EPS) * g + b


HALF_LANES = LANES // 2


def _rot_half(t):
    return pltpu.roll(t, HALF_LANES, 1)


def _split_rotary(t, d):
    lead = t.shape[:-1]
    t = t.reshape(lead + (2, 2, d // 4)).swapaxes(-3, -2).reshape(lead + (2, d // 2))
    t = jnp.pad(t, [(0, 0)] * (len(lead) + 1) + [(0, HALF_LANES - d // 2)])
    return t.reshape(lead + (LANES,))


def _rope_tables(d):
    t = jnp.arange(SEQ, dtype=jnp.int32)
    rows, cols = t // GRID_W, t % GRID_W
    quarter = d // 4
    freqs = ROPE_THETA ** (-jnp.arange(quarter, dtype=F32) / quarter)
    ang = jnp.concatenate([rows.astype(F32)[:, None] * freqs, cols.astype(F32)[:, None] * freqs], -1)
    c, s = jnp.cos(ang), jnp.sin(ang)
    pad = jnp.zeros((SEQ, HALF_LANES - d // 2), F32)
    return jnp.concatenate([c, pad, c, pad], -1), jnp.concatenate([-s, pad, s, pad], -1)


def _ada_body(c_ref, w_ref, b_ref, o_ref):
    c = c_ref[...]
    s = (c * (1.0 / (1.0 + jnp.exp(-c)))).astype(BF16)
    o_ref[0] = _dot(s, w_ref[0].astype(BF16)) + b_ref[0]


def _ada_all(cs, ada_w, ada_b):
    rows = cs.shape[0]
    tn = 1536
    return pl.pallas_call(
        _ada_body,
        out_shape=jax.ShapeDtypeStruct((DEPTH, rows, 6 * D_MODEL), F32),
        grid=(DEPTH, 6 * D_MODEL // tn),
        in_specs=[
            pl.BlockSpec((rows, D_MODEL), lambda i, n: (0, 0)),
            pl.BlockSpec((1, D_MODEL, tn), lambda i, n: (i, 0, n)),
            pl.BlockSpec((1, 1, tn), lambda i, n: (i, 0, n)),
        ],
        out_specs=pl.BlockSpec((1, rows, tn), lambda i, n: (i, 0, n)),
        compiler_params=_params(40, 2),
        name="ada_mod",
    )(cs, ada_w, ada_b.reshape(DEPTH, 1, 6 * D_MODEL))


def _tok_spec(width, tm=ROW_TILE):
    return pl.BlockSpec((1, tm, width), lambda b, t: (b, t, 0))


def _mod_spec():
    return pl.BlockSpec((1, 6, D_MODEL), lambda b, t: (b, 0, 0))


def _rope_spec(tm=ROW_TILE):
    return pl.BlockSpec((tm, LANES), lambda b, t: (t, 0))


PROJ_TILE = 1024
PROJ_CHAIN = 256
MLA_PROJ_CHAIN = 512


def _chains(ref, chain=PROJ_CHAIN):
    return [slice(r, r + chain) for r in range(0, ref.shape[1], chain)]


def _gqa_proj_body(*refs, rope):
    if rope:
        x_ref, m_ref, w_ref, qn_ref, kn_ref, cos_ref, sin_ref, q_ref, k_ref, v_ref = refs
    else:
        x_ref, m_ref, w_ref, qn_ref, kn_ref, q_ref, k_ref, v_ref = refs
    hd = GQA_HEAD_DIM
    gains = (qn_ref[...] * (hd ** -0.5 * LOG2E), kn_ref[...])
    for rows in _chains(x_ref):
        h = _modulate(x_ref[0, rows, :], m_ref, 0, 1).astype(BF16)
        qkv = _dot(h, w_ref[...])
        if rope:
            cos, sin = cos_ref[rows, :], sin_ref[rows, :]
            mix = [(cos * g, sin * _rot_half(g)) for g in gains]
        for i in range(GQA_Q_HEADS + GQA_KV_HEADS):
            is_q = i < GQA_Q_HEADS
            t = qkv[:, i * hd:(i + 1) * hd]
            t = t * lax.rsqrt(jnp.mean(t * t, -1, keepdims=True) + RMS_EPS)
            if rope:
                a, b = mix[0 if is_q else 1]
                t = t * a + _rot_half(t) * b
            else:
                t = t * gains[0 if is_q else 1]
            if is_q:
                q_ref[0, rows, i * hd:(i + 1) * hd] = t.astype(BF16)
            else:
                j = i - GQA_Q_HEADS
                k_ref[0, rows, j * hd:(j + 1) * hd] = t.astype(BF16)
        v_ref[0, rows, :] = qkv[:, (GQA_Q_HEADS + GQA_KV_HEADS) * hd:].astype(BF16)


def _gqa_proj(x, mod, w, qn, kn, tables):
    bx, sx, _ = x.shape
    rope = tables is not None
    nk = GQA_KV_HEADS * GQA_HEAD_DIM
    tm = min(PROJ_TILE, sx)
    ins = [x, mod, w, qn, kn]
    specs = [_tok_spec(D_MODEL, tm), _mod_spec(), _full_spec(w, 2), _full_spec(qn, 2),
             _full_spec(kn, 2)]
    if rope:
        ins += list(tables)
        specs += [_rope_spec(tm), _rope_spec(tm)]
    return pl.pallas_call(
        functools.partial(_gqa_proj_body, rope=rope),
        out_shape=(
            jax.ShapeDtypeStruct((bx, sx, D_MODEL), BF16),
            jax.ShapeDtypeStruct((bx, sx, nk), BF16),
            jax.ShapeDtypeStruct((bx, sx, nk), BF16),
        ),
        grid=(bx, sx // tm),
        in_specs=specs,
        out_specs=(_tok_spec(D_MODEL, tm), _tok_spec(nk, tm), _tok_spec(nk, tm)),
        compiler_params=_params(48, 2),
        name="gqa_proj",
    )(*ins)


def _mla_proj_body(*refs, rope, need_q):
    refs = list(refs)
    x_ref, m_ref = refs[:2]
    del refs[:2]
    if need_q:
        wdq_ref, qn_ref, wuq_ref = refs[:3]
        del refs[:3]
    wdkv_ref, kvn_ref, wukv_ref = refs[:3]
    del refs[:3]
    if rope:
        cos_ref, sin_ref = refs[:2]
        del refs[:2]
    if need_q:
        q_ref = refs.pop(0)
    k_ref, v_ref = refs

    scale = (MLA_NOPE + MLA_ROPE) ** -0.5 * LOG2E
    pw = MLA_QK_PAD
    for rows in _chains(x_ref, MLA_PROJ_CHAIN):
        h = _modulate(x_ref[0, rows, :], m_ref, 0, 1).astype(BF16)
        if rope:
            cos, sin = cos_ref[rows, :], sin_ref[rows, :]
            cos_q, sin_q = cos * scale, sin * scale

        if need_q:
            cq = _rms(_dot(h, wdq_ref[...]), qn_ref[...]).astype(BF16)
            q = _dot(cq, wuq_ref[...])
            for i in range(MLA_HEADS):
                nope = q[:, i * pw:i * pw + MLA_NOPE]
                pe = q[:, i * pw + MLA_NOPE:(i + 1) * pw]
                pe = pe * cos_q + _rot_half(pe) * sin_q if rope else pe * scale
                q_ref[0, rows, i * pw:i * pw + MLA_NOPE] = (nope * scale).astype(BF16)
                q_ref[0, rows, i * pw + MLA_NOPE:(i + 1) * pw] = pe.astype(BF16)

        ckv = _dot(h, wdkv_ref[...])
        c_kv = _rms(ckv[:, :MLA_KV_RANK], kvn_ref[...]).astype(BF16)
        k_pe = ckv[:, MLA_KV_RANK:]
        if rope:
            k_pe = k_pe * cos + _rot_half(k_pe) * sin
        k_pe = k_pe.astype(BF16)
        kv = _dot(c_kv, wukv_ref[...])
        for i in range(MLA_HEADS):
            k_ref[0, rows, i * pw:i * pw + MLA_NOPE] = (
                kv[:, i * MLA_NOPE:(i + 1) * MLA_NOPE].astype(BF16))
            k_ref[0, rows, i * pw + MLA_NOPE:(i + 1) * pw] = k_pe
        v_ref[0, rows, :] = kv[:, MLA_HEADS * MLA_NOPE:].astype(BF16)


def _mla_proj(x, mod, wq, wkv, tables, need_q):
    bx, sx, _ = x.shape
    rope = tables is not None
    tm = min(PROJ_TILE, sx)
    ins = [x, mod]
    specs = [_tok_spec(D_MODEL, tm), _mod_spec()]
    weights = (list(wq) if need_q else []) + list(wkv)
    ins += weights
    specs += [_full_spec(w, 2) for w in weights]
    if rope:
        ins += list(tables)
        specs += [_rope_spec(tm), _rope_spec(tm)]
    kw = MLA_HEADS * MLA_QK_PAD
    vw = MLA_HEADS * MLA_V
    shapes = [jax.ShapeDtypeStruct((bx, sx, kw), BF16), jax.ShapeDtypeStruct((bx, sx, vw), BF16)]
    ospecs = [_tok_spec(kw, tm), _tok_spec(vw, tm)]
    if need_q:
        shapes.insert(0, jax.ShapeDtypeStruct((bx, sx, kw), BF16))
        ospecs.insert(0, _tok_spec(kw, tm))
    out = pl.pallas_call(
        functools.partial(_mla_proj_body, rope=rope, need_q=need_q),
        out_shape=tuple(shapes),
        grid=(bx, sx // tm),
        in_specs=specs,
        out_specs=tuple(ospecs),
        compiler_params=_params(48, 2),
        name="mla_proj",
    )(*ins)
    return out if need_q else (None,) + tuple(out)


def _na_proj_body(x_ref, m_ref, w_ref, b_ref, q_ref, k_ref, v_ref):
    scale = NA_HEAD_DIM ** -0.5 * LOG2E
    for rows in _chains(x_ref):
        h = _modulate(x_ref[0, rows, :], m_ref, 0, 1).astype(BF16)
        qkv = _dot(h, w_ref[...]) + b_ref[...]
        q_ref[0, rows, :] = (qkv[:, :D_MODEL] * scale).astype(BF16)
        k_ref[0, rows, :] = qkv[:, D_MODEL:2 * D_MODEL].astype(BF16)
        v_ref[0, rows, :] = qkv[:, 2 * D_MODEL:].astype(BF16)


def _na_proj(x, mod, w, b):
    bx, sx, _ = x.shape
    tm = min(PROJ_TILE, sx)
    shape = jax.ShapeDtypeStruct((bx, sx, D_MODEL), BF16)
    return pl.pallas_call(
        _na_proj_body,
        out_shape=(shape, shape, shape),
        grid=(bx, sx // tm),
        in_specs=[_tok_spec(D_MODEL, tm), _mod_spec(), _full_spec(w, 2), _full_spec(b, 2)],
        out_specs=(_tok_spec(D_MODEL, tm),) * 3,
        compiler_params=_params(56, 2),
        name="na_proj",
    )(x, mod, w, b)


LOG2E = 1.4426950408889634


def _augment(v_block, sub, dv):
    col = lax.broadcasted_iota(jnp.int32, v_block.shape, 1)
    if dv == LANES:
        return jnp.concatenate([v_block, jnp.where(col == 0, 1.0, 0.0).astype(BF16)], -1)
    mine = (col < dv) if sub == 0 else (col >= dv)
    ones = jnp.where(col == (dv if sub == 0 else 0), 1.0, 0.0)
    return jnp.where(mine, v_block.astype(F32), ones).astype(BF16)


def _own_lanes(q_block, sub, dk):
    col = lax.broadcasted_iota(jnp.int32, q_block.shape, 1)
    mine = (col >= sub * dk) & (col < (sub + 1) * dk)
    return jnp.where(mine, q_block.astype(F32), 0.0).astype(BF16)


def _normalised(acc, sub, dv):
    ones_col = LANES if dv == LANES else (dv if sub == 0 else 0)
    return acc[:, :LANES] * (1.0 / acc[:, ones_col:ones_col + 1])


def _softmax_pv(q, keys, values, biases):
    scores = [_dot_nt(q, k) if b is None else _dot_nt(q, k) + b for k, b in zip(keys, biases)]
    m = functools.reduce(jnp.maximum, [jnp.max(s, -1, keepdims=True) for s in scores])
    return functools.reduce(
        jnp.add, [_dot(jnp.exp2(s - m).astype(BF16), v) for s, v in zip(scores, values)])


def _store_heads(o_ref, rows, block, outs, dv):
    if dv == LANES:
        for g, o in enumerate(outs):
            c0 = (block * len(outs) + g) * LANES
            o_ref[0, rows, c0:c0 + LANES] = o.astype(o_ref.dtype)
    else:
        col = lax.broadcasted_iota(jnp.int32, outs[0].shape, 1)
        pair = jnp.where(col < dv, outs[0], outs[1])
        o_ref[0, rows, block * LANES:(block + 1) * LANES] = pair.astype(o_ref.dtype)


def _attn_body(*refs, heads, group, dk, dv, has_lat, chain):
    if has_lat:
        q_ref, kc_ref, vc_ref, k_ref, v_ref, o_ref = refs
    else:
        q_ref, kc_ref, vc_ref, o_ref = refs
    tq = q_ref.shape[1]
    per_block = LANES // dv
    assert per_block == 1 or group == 1
    value_refs = [vc_ref, v_ref] if has_lat else [vc_ref]
    key_refs = [kc_ref, k_ref] if has_lat else [kc_ref]
    values = [[_augment(ref[0, :, (i // per_block) * LANES:(i // per_block + 1) * LANES],
                        i % per_block, dv) for ref in value_refs] for i in range(heads)]
    for r in range(tq // chain):
        rows = slice(r * chain, (r + 1) * chain)
        for block in range(heads // per_block):
            outs = []
            for sub in range(per_block):
                i = block * per_block + sub
                keys = [ref[0, :, i * dk:(i + 1) * dk] for ref in key_refs]
                for g in range(group):
                    c0 = (i * group + g) * dk
                    acc = _softmax_pv(q_ref[0, rows, c0:c0 + dk], keys, values[i],
                                      [None] * len(keys))
                    outs.append(_normalised(acc, sub, dv))
            _store_heads(o_ref, rows, block, outs, dv)


def _attention(q, kc, vc, k, v, *, heads, group, dk, dv, tq, chain=256):
    b, sq, qw = q.shape
    has_lat = k is not None
    n_blocks = qw // (heads * group * dk)
    qb, kb, vb = heads * group * dk, heads * dk, heads * dv
    ins = [q, kc, vc]
    specs = [
        pl.BlockSpec((1, tq, qb), lambda bi, hi, ti: (bi, ti, hi)),
        pl.BlockSpec((1, kc.shape[1], kb), lambda bi, hi, ti: (bi, 0, hi)),
        pl.BlockSpec((1, vc.shape[1], vb), lambda bi, hi, ti: (bi, 0, hi)),
    ]
    if has_lat:
        ins += [k, v]
        specs += [
            pl.BlockSpec((1, k.shape[1], kb), lambda bi, hi, ti: (bi, 0, hi)),
            pl.BlockSpec((1, v.shape[1], vb), lambda bi, hi, ti: (bi, 0, hi)),
        ]
    ob = heads * group * dv
    return pl.pallas_call(
        functools.partial(_attn_body, heads=heads, group=group, dk=dk, dv=dv, has_lat=has_lat,
                          chain=min(chain, tq)),
        out_shape=jax.ShapeDtypeStruct((b, sq, n_blocks * ob), BF16),
        grid=(b, n_blocks, sq // tq),
        in_specs=specs,
        out_specs=pl.BlockSpec((1, tq, ob), lambda bi, hi, ti: (bi, ti, hi)),
        compiler_params=_params(48, 3),
        name="attn_lat" if has_lat else "attn_ctx",
    )(*ins)


def _na_window(j):
    return min(max(NA_Q_ROWS * j - NA_WIN_ROWS // 2, 0), N_ROWS - NA_K_ROWS)


def _na_bias_pairs(rpb):
    c = jnp.arange(GRID_W, dtype=jnp.int32)
    col_start = jnp.clip(c - NA_WIN_COLS // 2, 0, GRID_W - NA_WIN_COLS)
    col_ok = (c[None, :] >= col_start[:, None]) & (c[None, :] < col_start[:, None] + NA_WIN_COLS)
    edge = GRID_W - NA_WIN_COLS
    ext = jnp.pad(rpb, ((0, 0), (0, 0), (edge, edge)), mode="edge")
    period = jnp.pad(ext, ((0, 0), (0, 0), (0, 1)))
    n = 2 * GRID_W - 1
    toep = jnp.tile(period, (1, 1, GRID_W))[..., :GRID_W * n]
    slab = toep.reshape(rpb.shape[0], rpb.shape[1], GRID_W, n)[..., GRID_W - 1:]
    slab = jnp.where(col_ok[None, None], slab * LOG2E, -jnp.inf)
    beyond = jnp.full_like(slab[:, :1], -jnp.inf)
    slab = jnp.concatenate([beyond, slab, beyond], 1)
    return jnp.concatenate([slab[:, :-1], slab[:, 1:]], -1)


def _na_bias_pieces(bias_ref, head, j, a, lane):
    w0 = _na_window(j)
    r = NA_Q_ROWS * j + a
    rs = min(max(r - NA_WIN_ROWS // 2, 0), N_ROWS - NA_WIN_ROWS)
    pieces = []
    for kr in range(w0, w0 + NA_K_ROWS, 2):
        first, second = (rs <= k < rs + NA_WIN_ROWS for k in (kr, kr + 1))
        if not (first or second):
            pieces.append(None)
            continue
        piece = bias_ref[head, kr - r + NA_WIN_ROWS]
        if not second:
            piece = jnp.where(lane < GRID_W, piece, -jnp.inf)
        elif not first:
            piece = jnp.where(lane >= GRID_W, piece, -jnp.inf)
        pieces.append(piece)
    return pieces


def _na_attn_body(q_ref, kc_ref, vc_ref, k_ref, v_ref, bias_ref, o_ref):
    hd = NA_HEAD_DIM
    heads = LANES // hd
    qn = NA_Q_ROWS * GRID_W
    vc_aug = _augment(vc_ref[0], 0, LANES)
    v_aug = _augment(v_ref[0], 0, LANES)
    lane = lax.broadcasted_iota(jnp.int32, (GRID_W, LANES), 1)
    no_prob = jnp.zeros((GRID_W, LANES), BF16)
    for j in range(N_ROWS // NA_Q_ROWS):
        w0 = _na_window(j)
        krows = slice(w0 * GRID_W, (w0 + NA_K_ROWS) * GRID_W)
        rows = slice(j * qn, (j + 1) * qn)
        q = jnp.concatenate([_own_lanes(q_ref[0, rows, :], i, hd) for i in range(heads)], 0)
        s_ctx = _dot_nt(q, kc_ref[0])
        s_lat = _dot_nt(q, k_ref[0, krows, :])
        p_ctx, p_lat = [], []
        for i in range(heads):
            for a in range(NA_Q_ROWS):
                qs = slice(i * qn + a * GRID_W, i * qn + (a + 1) * GRID_W)
                pieces = _na_bias_pieces(bias_ref, i, j, a, lane)
                sc = s_ctx[qs]
                sl = [None if b is None else s_lat[qs, p * LANES:(p + 1) * LANES] + b
                      for p, b in enumerate(pieces)]
                chunks = [sc[:, c0:c0 + LANES] for c0 in range(0, sc.shape[1], LANES)]
                chunks += [s for s in sl if s is not None]
                m = jnp.max(functools.reduce(jnp.maximum, chunks), -1, keepdims=True)
                p_ctx.append(jnp.exp2(sc - m).astype(BF16))
                p_lat.append(jnp.concatenate(
                    [no_prob if s is None else jnp.exp2(s - m).astype(BF16) for s in sl], -1))
        acc = (_dot(jnp.concatenate(p_ctx, 0), vc_aug)
               + _dot(jnp.concatenate(p_lat, 0), v_aug[krows]))
        outs = [_normalised(acc[i * qn:(i + 1) * qn], 0, LANES) for i in range(heads)]
        _store_heads(o_ref, rows, 0, outs, hd)


def _na_attention(q, kc, vc, k, v, bias):
    b, s, _ = q.shape
    n_pairs = D_MODEL // LANES
    heads = LANES // NA_HEAD_DIM
    lat = pl.BlockSpec((1, s, LANES), lambda hi, bi: (bi, 0, hi))
    ctx = pl.BlockSpec((1, CTX_LEN, LANES), lambda hi, bi: (bi, 0, hi))
    bias_spec = pl.BlockSpec((heads,) + bias.shape[1:], lambda hi, bi: (hi, 0, 0, 0))
    return pl.pallas_call(
        _na_attn_body,
        out_shape=jax.ShapeDtypeStruct((b, s, D_MODEL), BF16),
        grid=(n_pairs, b),
        in_specs=[lat, ctx, ctx, lat, lat, bias_spec],
        out_specs=lat,
        compiler_params=_params(48, 2),
        name="na_attn",
    )(q, kc, vc, k, v, bias)


MLP_CHUNK = 1024
TAIL_CHAIN = 512


def _tail_body(a_ref, wo_ref, x_ref, m_ref, g1_ref, b1_ref, w1_ref, w2_ref, g2_ref, b2_ref, o_ref):
    chains = _chains(x_ref, TAIL_CHAIN)
    x1s, hs = [], []
    for rows in chains:
        y = _dot(a_ref[0, rows, :], wo_ref[...])
        x1 = _layer_norm(ALPHA * x_ref[0, rows, :] + m_ref[0, 2:3, :] * y, g1_ref[...], b1_ref[...])
        x1s.append(x1)
        hs.append(_modulate(x1, m_ref, 3, 4).astype(BF16))
    accs = [None] * len(chains)
    for c in range(D_FF // MLP_CHUNK):
        cols = slice(c * MLP_CHUNK, (c + 1) * MLP_CHUNK)
        for r, h in enumerate(hs):
            u = jnp.square(jnp.maximum(_dot(h, w1_ref[:, cols]), 0.0)).astype(BF16)
            part = _dot(u, w2_ref[cols, :])
            accs[r] = part if accs[r] is None else accs[r] + part
    for rows, x1, acc in zip(chains, x1s, accs):
        z = ALPHA * x1 + m_ref[0, 5:6, :] * acc
        o_ref[0, rows, :] = _layer_norm(z, g2_ref[...], b2_ref[...])


def _layer_spec(stack, layer):
    return pl.BlockSpec((None,) + stack.shape[1:], lambda b, t: (layer, 0, 0),
                        pipeline_mode=pl.Buffered(1))


def _layer_tail(a, wo, x, mod, g1, b1, w1, w2, g2, b2, layer):
    bx, sx, _ = x.shape
    small = [_full_spec(g1, 2)] * 2
    tm = min(ROW_TILE, sx)
    return pl.pallas_call(
        _tail_body,
        out_shape=jax.ShapeDtypeStruct(x.shape, F32),
        grid=(bx, sx // tm),
        in_specs=[_tok_spec(a.shape[-1], tm), _full_spec(wo, 2, single=True),
                  _tok_spec(D_MODEL, tm), _mod_spec(), *small, _layer_spec(w1, layer),
                  _layer_spec(w2, layer), *small],
        out_specs=_tok_spec(D_MODEL, tm),
        compiler_params=_params(56, 2),
        name="layer_tail",
    )(a, wo, x, mod, g1, b1, w1, w2, g2, b2)


def _row(v):
    return v.reshape(1, -1)


def kernel(x, c, ctx, c_ctx, ada_w, ada_b, ln1_g, ln1_b, ln2_g, ln2_b, mlp_w1, mlp_w2, mla_w_dq, mla_q_norm, mla_w_uq, mla_w_dkv, mla_kv_norm, mla_w_ukv, mla_w_o, na_w_qkv, na_b_qkv, na_rpb, na_w_o, gqa_w_qkv, gqa_q_norm, gqa_k_norm, gqa_w_o):
    pad_rows = 24 - BATCH - 1
    cs = jnp.concatenate([c, c_ctx[None, :], jnp.zeros((pad_rows, D_MODEL), F32)], 0)
    mod_all = _ada_all(cs, ada_w, ada_b).reshape(DEPTH, 24, 6, D_MODEL)

    xc = ctx.reshape(1, BATCH * CTX_LEN, D_MODEL)
    gqa_tables = _rope_tables(GQA_HEAD_DIM)
    mla_tables = _rope_tables(MLA_ROPE)
    w1_all, w2_all = mlp_w1.astype(BF16), mlp_w2.astype(BF16)

    def per_batch(t):
        return t.reshape(BATCH, CTX_LEN, t.shape[-1])

    for i in range(DEPTH):
        kind, j = i % N_MIXERS, i // N_MIXERS
        need_ctx = i < DEPTH - 1
        mod = mod_all[i, :BATCH]
        mod_c = mod_all[i, BATCH:BATCH + 1]

        if kind == 0:
            wuq = mla_w_uq[j].reshape(MLA_Q_RANK, MLA_HEADS, MLA_NOPE + MLA_ROPE)
            wuq = jnp.concatenate(
                [wuq[..., :MLA_NOPE], _split_rotary(wuq[..., MLA_NOPE:], MLA_ROPE)], -1)
            wq = (mla_w_dq[j].astype(BF16), _row(mla_q_norm[j]),
                  wuq.reshape(MLA_Q_RANK, MLA_HEADS * MLA_QK_PAD).astype(BF16))
            wdkv = mla_w_dkv[j]
            wdkv = jnp.concatenate(
                [wdkv[:, :MLA_KV_RANK], _split_rotary(wdkv[:, MLA_KV_RANK:], MLA_ROPE)], -1)
            wukv = mla_w_ukv[j].reshape(MLA_KV_RANK, MLA_HEADS, 2, MLA_NOPE)
            wukv = wukv.transpose(0, 2, 1, 3).reshape(MLA_KV_RANK, 2 * MLA_HEADS * MLA_NOPE)
            wkv = (wdkv.astype(BF16), _row(mla_kv_norm[j]), wukv.astype(BF16))
            q, k, v = _mla_proj(x, mod, wq, wkv, mla_tables, True)
            qc, kc, vc = _mla_proj(xc, mod_c, wq, wkv, None, need_ctx)
            kc, vc = per_batch(kc), per_batch(vc)
            dims = dict(heads=1, group=1, dk=MLA_QK_PAD, dv=MLA_V)
            y = _attention(q, kc, vc, k, v, tq=SEQ, **{**dims, "heads": 2})
            if need_ctx:
                yc = _attention(per_batch(qc), kc, vc, None, None, tq=CTX_LEN,
                                **{**dims, "heads": MLA_HEADS})
            w_o = mla_w_o[j]
        elif kind == 1:
            w, b = na_w_qkv[j].astype(BF16), _row(na_b_qkv[j])
            q, k, v = _na_proj(x, mod, w, b)
            qc, kc, vc = _na_proj(xc, mod_c, w, b)
            qc, kc, vc = per_batch(qc), per_batch(kc), per_batch(vc)
            y = _na_attention(q, kc, vc, k, v, _na_bias_pairs(na_rpb[j]))
            if need_ctx:
                yc = _attention(qc, kc, vc, None, None, tq=CTX_LEN, heads=NA_HEADS,
                                group=1, dk=NA_HEAD_DIM, dv=NA_HEAD_DIM)
            w_o = na_w_o[j]
        else:
            n_rot = (GQA_Q_HEADS + GQA_KV_HEADS) * GQA_HEAD_DIM
            w = gqa_w_qkv[j]
            w_rot = w[:, :n_rot].reshape(D_MODEL, GQA_Q_HEADS + GQA_KV_HEADS, GQA_HEAD_DIM)
            w_rot = _split_rotary(w_rot, GQA_HEAD_DIM).reshape(D_MODEL, n_rot)
            w = jnp.concatenate([w_rot, w[:, n_rot:]], -1).astype(BF16)
            qn = _row(_split_rotary(gqa_q_norm[j], GQA_HEAD_DIM))
            kn = _row(_split_rotary(gqa_k_norm[j], GQA_HEAD_DIM))
            q, k, v = _gqa_proj(x, mod, w, qn, kn, gqa_tables)
            qc, kc, vc = _gqa_proj(xc, mod_c, w, qn, kn, None)
            qc, kc, vc = per_batch(qc), per_batch(kc), per_batch(vc)
            dims = dict(heads=1, group=GQA_GROUP, dk=GQA_HEAD_DIM, dv=GQA_HEAD_DIM)
            y = _attention(q, kc, vc, k, v, tq=1024, **dims)
            if need_ctx:
                yc = _attention(qc, kc, vc, None, None, tq=CTX_LEN,
                                **{**dims, "heads": GQA_KV_HEADS})
            w_o = gqa_w_o[j]

        w_o = w_o.astype(BF16)
        g1, b1, g2, b2 = _row(ln1_g[i]), _row(ln1_b[i]), _row(ln2_g[i]), _row(ln2_b[i])
        x = _layer_tail(y, w_o, x, mod, g1, b1, w1_all, w2_all, g2, b2, i)
        if need_ctx:
            yc = yc.reshape(1, BATCH * CTX_LEN, D_MODEL)
            xc = _layer_tail(yc, w_o, xc, mod_c, g1, b1, w1_all, w2_all, g2, b2, i)
    return x
```

```python
import functools

import jax
import jax.numpy as jnp
from jax import lax
from jax.experimental import pallas as pl
from jax.experimental.pallas import tpu as pltpu

D_MODEL = 1024
BATCH = 16
SEQ = 2048
DEPTH = 4
GRID_W = 64
CTX_LEN = 256
N_MIXERS = 3
ROPE_THETA = 10000.0
LN_EPS = 1e-5
RMS_EPS = 1e-6
D_FF = 4 * D_MODEL

MLA_HEADS = 8
MLA_Q_RANK = 512
MLA_KV_RANK = 256
MLA_NOPE = 128
MLA_ROPE = 64
MLA_V = 128
MLA_QK_PAD = 256

NA_HEADS = 16
NA_HEAD_DIM = D_MODEL // NA_HEADS
NA_WIN_ROWS = 8
NA_WIN_COLS = 16
NA_Q_ROWS = 4
NA_K_ROWS = 12
N_ROWS = SEQ // GRID_W

GQA_HEAD_DIM = 128
GQA_Q_HEADS = D_MODEL // GQA_HEAD_DIM
GQA_KV_HEADS = GQA_Q_HEADS // 4
GQA_GROUP = GQA_Q_HEADS // GQA_KV_HEADS

ALPHA = (2.0 * DEPTH) ** 0.25
LANES = 128
ROW_TILE = 1024
F32 = jnp.float32
BF16 = jnp.bfloat16
MIB = 1024 * 1024


def _params(vmem_mib, n_axes):
    return pltpu.CompilerParams(
        dimension_semantics=("arbitrary",) * n_axes, vmem_limit_bytes=vmem_mib * MIB
    )


def _full_spec(arr, n_axes, single=False):
    zeros = (0,) * arr.ndim
    index_map = {2: lambda a, b: zeros, 3: lambda a, b, c: zeros}[n_axes]
    if single:
        return pl.BlockSpec(arr.shape, index_map, pipeline_mode=pl.Buffered(1))
    return pl.BlockSpec(arr.shape, index_map)


def _dot(a, b):
    return jnp.dot(a, b, preferred_element_type=F32)


def _dot_nt(a, b):
    return lax.dot_general(a, b, (((1,), (1,)), ((), ())), preferred_element_type=F32)


def _modulate(x, m_ref, shift_row, scale_row):
    shift = m_ref[0, shift_row:shift_row + 1, :]
    scale = m_ref[0, scale_row:scale_row + 1, :]
    return x * (1.0 + scale) + shift


def _rms(t, g):
    return t * lax.rsqrt(jnp.mean(t * t, -1, keepdims=True) + RMS_EPS) * g


def _layer_norm(z, g, b):
    mu = jnp.mean(z, -1, keepdims=True)
    zc = z - mu
    var = jnp.mean(zc * zc, -1, keepdims=True)
    return zc * lax.rsqrt(var + LN_EPS) * g + b


HALF_LANES = LANES // 2


def _rot_half(t):
    return pltpu.roll(t, HALF_LANES, 1)


def _split_rotary(t, d):
    lead = t.shape[:-1]
    t = t.reshape(lead + (2, 2, d // 4)).swapaxes(-3, -2).reshape(lead + (2, d // 2))
    t = jnp.pad(t, [(0, 0)] * (len(lead) + 1) + [(0, HALF_LANES - d // 2)])
    return t.reshape(lead + (LANES,))


def _rope_tables(d):
    t = jnp.arange(SEQ, dtype=jnp.int32)
    rows, cols = t // GRID_W, t % GRID_W
    quarter = d // 4
    freqs = ROPE_THETA ** (-jnp.arange(quarter, dtype=F32) / quarter)
    ang = jnp.concatenate([rows.astype(F32)[:, None] * freqs, cols.astype(F32)[:, None] * freqs], -1)
    c, s = jnp.cos(ang), jnp.sin(ang)
    pad = jnp.zeros((SEQ, HALF_LANES - d // 2), F32)
    return jnp.concatenate([c, pad, c, pad], -1), jnp.concatenate([-s, pad, s, pad], -1)


def _ada_body(c_ref, w_ref, b_ref, o_ref):
    c = c_ref[...]
    s = (c * (1.0 / (1.0 + jnp.exp(-c)))).astype(BF16)
    o_ref[0] = _dot(s, w_ref[0].astype(BF16)) + b_ref[0]


def _ada_all(cs, ada_w, ada_b):
    rows = cs.shape[0]
    tn = 1536
    return pl.pallas_call(
        _ada_body,
        out_shape=jax.ShapeDtypeStruct((DEPTH, rows, 6 * D_MODEL), F32),
        grid=(DEPTH, 6 * D_MODEL // tn),
        in_specs=[
            pl.BlockSpec((rows, D_MODEL), lambda i, n: (0, 0)),
            pl.BlockSpec((1, D_MODEL, tn), lambda i, n: (i, 0, n)),
            pl.BlockSpec((1, 1, tn), lambda i, n: (i, 0, n)),
        ],
        out_specs=pl.BlockSpec((1, rows, tn), lambda i, n: (i, 0, n)),
        compiler_params=_params(40, 2),
        name="ada_mod",
    )(cs, ada_w, ada_b.reshape(DEPTH, 1, 6 * D_MODEL))


def _tok_spec(width, tm=ROW_TILE):
    return pl.BlockSpec((1, tm, width), lambda b, t: (b, t, 0))


def _mod_spec():
    return pl.BlockSpec((1, 6, D_MODEL), lambda b, t: (b, 0, 0))


def _rope_spec(tm=ROW_TILE):
    return pl.BlockSpec((tm, LANES), lambda b, t: (t, 0))


PROJ_TILE = 1024
PROJ_CHAIN = 256
MLA_PROJ_CHAIN = 512


def _chains(ref, chain=PROJ_CHAIN):
    return [slice(r, r + chain) for r in range(0, ref.shape[1], chain)]


def _gqa_proj_body(*refs, rope):
    if rope:
        x_ref, m_ref, w_ref, qn_ref, kn_ref, cos_ref, sin_ref, q_ref, k_ref, v_ref = refs
    else:
        x_ref, m_ref, w_ref, qn_ref, kn_ref, q_ref, k_ref, v_ref = refs
    hd = GQA_HEAD_DIM
    gains = (qn_ref[...] * (hd ** -0.5 * LOG2E), kn_ref[...])
    for rows in _chains(x_ref):
        h = _modulate(x_ref[0, rows, :], m_ref, 0, 1).astype(BF16)
        qkv = _dot(h, w_ref[...])
        if rope:
            cos, sin = cos_ref[rows, :], sin_ref[rows, :]
            mix = [(cos * g, sin * _rot_half(g)) for g in gains]
        for i in range(GQA_Q_HEADS + GQA_KV_HEADS):
            is_q = i < GQA_Q_HEADS
            t = qkv[:, i * hd:(i + 1) * hd]
            t = t * lax.rsqrt(jnp.mean(t * t, -1, keepdims=True) + RMS_EPS)
            if rope:
                a, b = mix[0 if is_q else 1]
                t = t * a + _rot_half(t) * b
            else:
                t = t * gains[0 if is_q else 1]
            if is_q:
                q_ref[0, rows, i * hd:(i + 1) * hd] = t.astype(BF16)
            else:
                j = i - GQA_Q_HEADS
                k_ref[0, rows, j * hd:(j + 1) * hd] = t.astype(BF16)
        v_ref[0, rows, :] = qkv[:, (GQA_Q_HEADS + GQA_KV_HEADS) * hd:].astype(BF16)


def _gqa_proj(x, mod, w, qn, kn, tables):
    bx, sx, _ = x.shape
    rope = tables is not None
    nk = GQA_KV_HEADS * GQA_HEAD_DIM
    tm = min(PROJ_TILE, sx)
    ins = [x, mod, w, qn, kn]
    specs = [_tok_spec(D_MODEL, tm), _mod_spec(), _full_spec(w, 2), _full_spec(qn, 2),
             _full_spec(kn, 2)]
    if rope:
        ins += list(tables)
        specs += [_rope_spec(tm), _rope_spec(tm)]
    return pl.pallas_call(
        functools.partial(_gqa_proj_body, rope=rope),
        out_shape=(
            jax.ShapeDtypeStruct((bx, sx, D_MODEL), BF16),
            jax.ShapeDtypeStruct((bx, sx, nk), BF16),
            jax.ShapeDtypeStruct((bx, sx, nk), BF16),
        ),
        grid=(bx, sx // tm),
        in_specs=specs,
        out_specs=(_tok_spec(D_MODEL, tm), _tok_spec(nk, tm), _tok_spec(nk, tm)),
        compiler_params=_params(48, 2),
        name="gqa_proj",
    )(*ins)


def _mla_proj_body(*refs, rope, need_q):
    refs = list(refs)
    x_ref, m_ref = refs[:2]
    del refs[:2]
    if need_q:
        wdq_ref, qn_ref, wuq_ref = refs[:3]
        del refs[:3]
    wdkv_ref, kvn_ref, wukv_ref = refs[:3]
    del refs[:3]
    if rope:
        cos_ref, sin_ref = refs[:2]
        del refs[:2]
    if need_q:
        q_ref = refs.pop(0)
    k_ref, v_ref = refs

    scale = (MLA_NOPE + MLA_ROPE) ** -0.5 * LOG2E
    pw = MLA_QK_PAD
    for rows in _chains(x_ref, MLA_PROJ_CHAIN):
        h = _modulate(x_ref[0, rows, :], m_ref, 0, 1).astype(BF16)
        if rope:
            cos, sin = cos_ref[rows, :], sin_ref[rows, :]
            cos_q, sin_q = cos * scale, sin * scale

        if need_q:
            cq = _rms(_dot(h, wdq_ref[...]), qn_ref[...]).astype(BF16)
            q = _dot(cq, wuq_ref[...])
            for i in range(MLA_HEADS):
                nope = q[:, i * pw:i * pw + MLA_NOPE]
                pe = q[:, i * pw + MLA_NOPE:(i + 1) * pw]
                pe = pe * cos_q + _rot_half(pe) * sin_q if rope else pe * scale
                q_ref[0, rows, i * pw:i * pw + MLA_NOPE] = (nope * scale).astype(BF16)
                q_ref[0, rows, i * pw + MLA_NOPE:(i + 1) * pw] = pe.astype(BF16)

        ckv = _dot(h, wdkv_ref[...])
        c_kv = _rms(ckv[:, :MLA_KV_RANK], kvn_ref[...]).astype(BF16)
        k_pe = ckv[:, MLA_KV_RANK:]
        if rope:
            k_pe = k_pe * cos + _rot_half(k_pe) * sin
        k_pe = k_pe.astype(BF16)
        kv = _dot(c_kv, wukv_ref[...])
        for i in range(MLA_HEADS):
            k_ref[0, rows, i * pw:i * pw + MLA_NOPE] = (
                kv[:, i * MLA_NOPE:(i + 1) * MLA_NOPE].astype(BF16))
            k_ref[0, rows, i * pw + MLA_NOPE:(i + 1) * pw] = k_pe
        v_ref[0, rows, :] = kv[:, MLA_HEADS * MLA_NOPE:].astype(BF16)


def _mla_proj(x, mod, wq, wkv, tables, need_q):
    bx, sx, _ = x.shape
    rope = tables is not None
    tm = min(PROJ_TILE, sx)
    ins = [x, mod]
    specs = [_tok_spec(D_MODEL, tm), _mod_spec()]
    weights = (list(wq) if need_q else []) + list(wkv)
    ins += weights
    specs += [_full_spec(w, 2) for w in weights]
    if rope:
        ins += list(tables)
        specs += [_rope_spec(tm), _rope_spec(tm)]
    kw = MLA_HEADS * MLA_QK_PAD
    vw = MLA_HEADS * MLA_V
    shapes = [jax.ShapeDtypeStruct((bx, sx, kw), BF16), jax.ShapeDtypeStruct((bx, sx, vw), BF16)]
    ospecs = [_tok_spec(kw, tm), _tok_spec(vw, tm)]
    if need_q:
        shapes.insert(0, jax.ShapeDtypeStruct((bx, sx, kw), BF16))
        ospecs.insert(0, _tok_spec(kw, tm))
    out = pl.pallas_call(
        functools.partial(_mla_proj_body, rope=rope, need_q=need_q),
        out_shape=tuple(shapes),
        grid=(bx, sx // tm),
        in_specs=specs,
        out_specs=tuple(ospecs),
        compiler_params=_params(48, 2),
        name="mla_proj",
    )(*ins)
    return out if need_q else (None,) + tuple(out)


def _na_proj_body(x_ref, m_ref, w_ref, b_ref, q_ref, k_ref, v_ref):
    scale = NA_HEAD_DIM ** -0.5 * LOG2E
    for rows in _chains(x_ref):
        h = _modulate(x_ref[0, rows, :], m_ref, 0, 1).astype(BF16)
        qkv = _dot(h, w_ref[...]) + b_ref[...]
        q_ref[0, rows, :] = (qkv[:, :D_MODEL] * scale).astype(BF16)
        k_ref[0, rows, :] = qkv[:, D_MODEL:2 * D_MODEL].astype(BF16)
        v_ref[0, rows, :] = qkv[:, 2 * D_MODEL:].astype(BF16)


def _na_proj(x, mod, w, b):
    bx, sx, _ = x.shape
    tm = min(PROJ_TILE, sx)
    shape = jax.ShapeDtypeStruct((bx, sx, D_MODEL), BF16)
    return pl.pallas_call(
        _na_proj_body,
        out_shape=(shape, shape, shape),
        grid=(bx, sx // tm),
        in_specs=[_tok_spec(D_MODEL, tm), _mod_spec(), _full_spec(w, 2), _full_spec(b, 2)],
        out_specs=(_tok_spec(D_MODEL, tm),) * 3,
        compiler_params=_params(56, 2),
        name="na_proj",
    )(x, mod, w, b)


LOG2E = 1.4426950408889634


def _augment(v_block, sub, dv):
    col = lax.broadcasted_iota(jnp.int32, v_block.shape, 1)
    if dv == LANES:
        return jnp.concatenate([v_block, jnp.where(col == 0, 1.0, 0.0).astype(BF16)], -1)
    mine = (col < dv) if sub == 0 else (col >= dv)
    ones = jnp.where(col == (dv if sub == 0 else 0), 1.0, 0.0)
    return jnp.where(mine, v_block.astype(F32), ones).astype(BF16)


def _own_lanes(q_block, sub, dk):
    col = lax.broadcasted_iota(jnp.int32, q_block.shape, 1)
    mine = (col >= sub * dk) & (col < (sub + 1) * dk)
    return jnp.where(mine, q_block.astype(F32), 0.0).astype(BF16)


def _normalised(acc, sub, dv):
    ones_col = LANES if dv == LANES else (dv if sub == 0 else 0)
    return acc[:, :LANES] * (1.0 / acc[:, ones_col:ones_col + 1])


def _softmax_pv(q, keys, values, biases):
    scores = [_dot_nt(q, k) if b is None else _dot_nt(q, k) + b for k, b in zip(keys, biases)]
    m = functools.reduce(jnp.maximum, [jnp.max(s, -1, keepdims=True) for s in scores])
    return functools.reduce(
        jnp.add, [_dot(jnp.exp2(s - m).astype(BF16), v) for s, v in zip(scores, values)])


def _store_heads(o_ref, rows, block, outs, dv):
    if dv == LANES:
        for g, o in enumerate(outs):
            c0 = (block * len(outs) + g) * LANES
            o_ref[0, rows, c0:c0 + LANES] = o.astype(o_ref.dtype)
    else:
        col = lax.broadcasted_iota(jnp.int32, outs[0].shape, 1)
        pair = jnp.where(col < dv, outs[0], outs[1])
        o_ref[0, rows, block * LANES:(block + 1) * LANES] = pair.astype(o_ref.dtype)


def _attn_body(*refs, heads, group, dk, dv, has_lat, chain):
    if has_lat:
        q_ref, kc_ref, vc_ref, k_ref, v_ref, o_ref = refs
    else:
        q_ref, kc_ref, vc_ref, o_ref = refs
    tq = q_ref.shape[1]
    per_block = LANES // dv
    assert per_block == 1 or group == 1
    value_refs = [vc_ref, v_ref] if has_lat else [vc_ref]
    key_refs = [kc_ref, k_ref] if has_lat else [kc_ref]
    values = [[_augment(ref[0, :, (i // per_block) * LANES:(i // per_block + 1) * LANES],
                        i % per_block, dv) for ref in value_refs] for i in range(heads)]
    for r in range(tq // chain):
        rows = slice(r * chain, (r + 1) * chain)
        for block in range(heads // per_block):
            outs = []
            for sub in range(per_block):
                i = block * per_block + sub
                keys = [ref[0, :, i * dk:(i + 1) * dk] for ref in key_refs]
                for g in range(group):
                    c0 = (i * group + g) * dk
                    acc = _softmax_pv(q_ref[0, rows, c0:c0 + dk], keys, values[i],
                                      [None] * len(keys))
                    outs.append(_normalised(acc, sub, dv))
            _store_heads(o_ref, rows, block, outs, dv)


def _attention(q, kc, vc, k, v, *, heads, group, dk, dv, tq, chain=256):
    b, sq, qw = q.shape
    has_lat = k is not None
    n_blocks = qw // (heads * group * dk)
    qb, kb, vb = heads * group * dk, heads * dk, heads * dv
    ins = [q, kc, vc]
    specs = [
        pl.BlockSpec((1, tq, qb), lambda bi, hi, ti: (bi, ti, hi)),
        pl.BlockSpec((1, kc.shape[1], kb), lambda bi, hi, ti: (bi, 0, hi)),
        pl.BlockSpec((1, vc.shape[1], vb), lambda bi, hi, ti: (bi, 0, hi)),
    ]
    if has_lat:
        ins += [k, v]
        specs += [
            pl.BlockSpec((1, k.shape[1], kb), lambda bi, hi, ti: (bi, 0, hi)),
            pl.BlockSpec((1, v.shape[1], vb), lambda bi, hi, ti: (bi, 0, hi)),
        ]
    ob = heads * group * dv
    return pl.pallas_call(
        functools.partial(_attn_body, heads=heads, group=group, dk=dk, dv=dv, has_lat=has_lat,
                          chain=min(chain, tq)),
        out_shape=jax.ShapeDtypeStruct((b, sq, n_blocks * ob), BF16),
        grid=(b, n_blocks, sq // tq),
        in_specs=specs,
        out_specs=pl.BlockSpec((1, tq, ob), lambda bi, hi, ti: (bi, ti, hi)),
        compiler_params=_params(48, 3),
        name="attn_lat" if has_lat else "attn_ctx",
    )(*ins)


def _na_window(j):
    return min(max(NA_Q_ROWS * j - NA_WIN_ROWS // 2, 0), N_ROWS - NA_K_ROWS)


def _na_bias_pairs(rpb):
    c = jnp.arange(GRID_W, dtype=jnp.int32)
    col_start = jnp.clip(c - NA_WIN_COLS // 2, 0, GRID_W - NA_WIN_COLS)
    col_ok = (c[None, :] >= col_start[:, None]) & (c[None, :] < col_start[:, None] + NA_WIN_COLS)
    edge = GRID_W - NA_WIN_COLS
    ext = jnp.pad(rpb, ((0, 0), (0, 0), (edge, edge)), mode="edge")
    period = jnp.pad(ext, ((0, 0), (0, 0), (0, 1)))
    n = 2 * GRID_W - 1
    toep = jnp.tile(period, (1, 1, GRID_W))[..., :GRID_W * n]
    slab = toep.reshape(rpb.shape[0], rpb.shape[1], GRID_W, n)[..., GRID_W - 1:]
    slab = jnp.where(col_ok[None, None], slab * LOG2E, -jnp.inf)
    beyond = jnp.full_like(slab[:, :1], -jnp.inf)
    slab = jnp.concatenate([beyond, slab, beyond], 1)
    return jnp.concatenate([slab[:, :-1], slab[:, 1:]], -1)


def _na_bias_pieces(bias_ref, head, j, a, lane):
    w0 = _na_window(j)
    r = NA_Q_ROWS * j + a
    rs = min(max(r - NA_WIN_ROWS // 2, 0), N_ROWS - NA_WIN_ROWS)
    pieces = []
    for kr in range(w0, w0 + NA_K_ROWS, 2):
        first, second = (rs <= k < rs + NA_WIN_ROWS for k in (kr, kr + 1))
        if not (first or second):
            pieces.append(None)
            continue
        piece = bias_ref[head, kr - r + NA_WIN_ROWS]
        if not second:
            piece = jnp.where(lane < GRID_W, piece, -jnp.inf)
        elif not first:
            piece = jnp.where(lane >= GRID_W, piece, -jnp.inf)
        pieces.append(piece)
    return pieces


def _na_attn_body(q_ref, kc_ref, vc_ref, k_ref, v_ref, bias_ref, o_ref):
    hd = NA_HEAD_DIM
    heads = LANES // hd
    qn = NA_Q_ROWS * GRID_W
    vc_aug = _augment(vc_ref[0], 0, LANES)
    v_aug = _augment(v_ref[0], 0, LANES)
    lane = lax.broadcasted_iota(jnp.int32, (GRID_W, LANES), 1)
    no_prob = jnp.zeros((GRID_W, LANES), BF16)
    for j in range(N_ROWS // NA_Q_ROWS):
        w0 = _na_window(j)
        krows = slice(w0 * GRID_W, (w0 + NA_K_ROWS) * GRID_W)
        rows = slice(j * qn, (j + 1) * qn)
        q = jnp.concatenate([_own_lanes(q_ref[0, rows, :], i, hd) for i in range(heads)], 0)
        s_ctx = _dot_nt(q, kc_ref[0])
        s_lat = _dot_nt(q, k_ref[0, krows, :])
        p_ctx, p_lat = [], []
        for i in range(heads):
            for a in range(NA_Q_ROWS):
                qs = slice(i * qn + a * GRID_W, i * qn + (a + 1) * GRID_W)
                pieces = _na_bias_pieces(bias_ref, i, j, a, lane)
                sc = s_ctx[qs]
                sl = [None if b is None else s_lat[qs, p * LANES:(p + 1) * LANES] + b
                      for p, b in enumerate(pieces)]
                chunks = [sc[:, c0:c0 + LANES] for c0 in range(0, sc.shape[1], LANES)]
                chunks += [s for s in sl if s is not None]
                m = jnp.max(functools.reduce(jnp.maximum, chunks), -1, keepdims=True)
                p_ctx.append(jnp.exp2(sc - m).astype(BF16))
                p_lat.append(jnp.concatenate(
                    [no_prob if s is None else jnp.exp2(s - m).astype(BF16) for s in sl], -1))
        acc = (_dot(jnp.concatenate(p_ctx, 0), vc_aug)
               + _dot(jnp.concatenate(p_lat, 0), v_aug[krows]))
        outs = [_normalised(acc[i * qn:(i + 1) * qn], 0, LANES) for i in range(heads)]
        _store_heads(o_ref, rows, 0, outs, hd)


def _na_attention(q, kc, vc, k, v, bias):
    b, s, _ = q.shape
    n_pairs = D_MODEL // LANES
    heads = LANES // NA_HEAD_DIM
    lat = pl.BlockSpec((1, s, LANES), lambda hi, bi: (bi, 0, hi))
    ctx = pl.BlockSpec((1, CTX_LEN, LANES), lambda hi, bi: (bi, 0, hi))
    bias_spec = pl.BlockSpec((heads,) + bias.shape[1:], lambda hi, bi: (hi, 0, 0, 0))
    return pl.pallas_call(
        _na_attn_body,
        out_shape=jax.ShapeDtypeStruct((b, s, D_MODEL), BF16),
        grid=(n_pairs, b),
        in_specs=[lat, ctx, ctx, lat, lat, bias_spec],
        out_specs=lat,
        compiler_params=_params(48, 2),
        name="na_attn",
    )(q, kc, vc, k, v, bias)


MLP_CHUNK = 1024
TAIL_CHAIN = 512


def _tail_body(a_ref, wo_ref, x_ref, m_ref, g1_ref, b1_ref, w1_ref, w2_ref, g2_ref, b2_ref, o_ref):
    chains = _chains(x_ref, TAIL_CHAIN)
    x1s, hs = [], []
    for rows in chains:
        y = _dot(a_ref[0, rows, :], wo_ref[...])
        x1 = _layer_norm(ALPHA * x_ref[0, rows, :] + m_ref[0, 2:3, :] * y, g1_ref[...], b1_ref[...])
        x1s.append(x1)
        hs.append(_modulate(x1, m_ref, 3, 4).astype(BF16))
    accs = [None] * len(chains)
    for c in range(D_FF // MLP_CHUNK):
        cols = slice(c * MLP_CHUNK, (c + 1) * MLP_CHUNK)
        for r, h in enumerate(hs):
            u = jnp.square(jnp.maximum(_dot(h, w1_ref[:, cols]), 0.0)).astype(BF16)
            part = _dot(u, w2_ref[cols, :])
            accs[r] = part if accs[r] is None else accs[r] + part
    for rows, x1, acc in zip(chains, x1s, accs):
        z = ALPHA * x1 + m_ref[0, 5:6, :] * acc
        o_ref[0, rows, :] = _layer_norm(z, g2_ref[...], b2_ref[...])


def _layer_spec(stack, layer):
    return pl.BlockSpec((None,) + stack.shape[1:], lambda b, t: (layer, 0, 0),
                        pipeline_mode=pl.Buffered(1))


def _layer_tail(a, wo, x, mod, g1, b1, w1, w2, g2, b2, layer):
    bx, sx, _ = x.shape
    small = [_full_spec(g1, 2)] * 2
    tm = min(ROW_TILE, sx)
    return pl.pallas_call(
        _tail_body,
        out_shape=jax.ShapeDtypeStruct(x.shape, F32),
        grid=(bx, sx // tm),
        in_specs=[_tok_spec(a.shape[-1], tm), _full_spec(wo, 2, single=True),
                  _tok_spec(D_MODEL, tm), _mod_spec(), *small, _layer_spec(w1, layer),
                  _layer_spec(w2, layer), *small],
        out_specs=_tok_spec(D_MODEL, tm),
        compiler_params=_params(56, 2),
        name="layer_tail",
    )(a, wo, x, mod, g1, b1, w1, w2, g2, b2)


def _row(v):
    return v.reshape(1, -1)


def kernel(x, c, ctx, c_ctx, ada_w, ada_b, ln1_g, ln1_b, ln2_g, ln2_b, mlp_w1, mlp_w2, mla_w_dq, mla_q_norm, mla_w_uq, mla_w_dkv, mla_kv_norm, mla_w_ukv, mla_w_o, na_w_qkv, na_b_qkv, na_rpb, na_w_o, gqa_w_qkv, gqa_q_norm, gqa_k_norm, gqa_w_o):
    pad_rows = 24 - BATCH - 1
    cs = jnp.concatenate([c, c_ctx[None, :], jnp.zeros((pad_rows, D_MODEL), F32)], 0)
    mod_all = _ada_all(cs, ada_w, ada_b).reshape(DEPTH, 24, 6, D_MODEL)

    xc = ctx.reshape(1, BATCH * CTX_LEN, D_MODEL)
    gqa_tables = _rope_tables(GQA_HEAD_DIM)
    mla_tables = _rope_tables(MLA_ROPE)
    w1_all, w2_all = mlp_w1.astype(BF16), mlp_w2.astype(BF16)

    def per_batch(t):
        return t.reshape(BATCH, CTX_LEN, t.shape[-1])

    for i in range(DEPTH):
        kind, j = i % N_MIXERS, i // N_MIXERS
        need_ctx = i < DEPTH - 1
        mod = mod_all[i, :BATCH]
        mod_c = mod_all[i, BATCH:BATCH + 1]

        if kind == 0:
            wuq = mla_w_uq[j].reshape(MLA_Q_RANK, MLA_HEADS, MLA_NOPE + MLA_ROPE)
            wuq = jnp.concatenate(
                [wuq[..., :MLA_NOPE], _split_rotary(wuq[..., MLA_NOPE:], MLA_ROPE)], -1)
            wq = (mla_w_dq[j].astype(BF16), _row(mla_q_norm[j]),
                  wuq.reshape(MLA_Q_RANK, MLA_HEADS * MLA_QK_PAD).astype(BF16))
            wdkv = mla_w_dkv[j]
            wdkv = jnp.concatenate(
                [wdkv[:, :MLA_KV_RANK], _split_rotary(wdkv[:, MLA_KV_RANK:], MLA_ROPE)], -1)
            wukv = mla_w_ukv[j].reshape(MLA_KV_RANK, MLA_HEADS, 2, MLA_NOPE)
            wukv = wukv.transpose(0, 2, 1, 3).reshape(MLA_KV_RANK, 2 * MLA_HEADS * MLA_NOPE)
            wkv = (wdkv.astype(BF16), _row(mla_kv_norm[j]), wukv.astype(BF16))
            q, k, v = _mla_proj(x, mod, wq, wkv, mla_tables, True)
            qc, kc, vc = _mla_proj(xc, mod_c, wq, wkv, None, need_ctx)
            kc, vc = per_batch(kc), per_batch(vc)
            dims = dict(heads=1, group=1, dk=MLA_QK_PAD, dv=MLA_V)
            y = _attention(q, kc, vc, k, v, tq=SEQ, **{**dims, "heads": 4})
            if need_ctx:
                yc = _attention(per_batch(qc), kc, vc, None, None, tq=CTX_LEN,
                                **{**dims, "heads": MLA_HEADS})
            w_o = mla_w_o[j]
        elif kind == 1:
            w, b = na_w_qkv[j].astype(BF16), _row(na_b_qkv[j])
            q, k, v = _na_proj(x, mod, w, b)
            qc, kc, vc = _na_proj(xc, mod_c, w, b)
            qc, kc, vc = per_batch(qc), per_batch(kc), per_batch(vc)
            y = _na_attention(q, kc, vc, k, v, _na_bias_pairs(na_rpb[j]))
            if need_ctx:
                yc = _attention(qc, kc, vc, None, None, tq=CTX_LEN, heads=NA_HEADS,
                                group=1, dk=NA_HEAD_DIM, dv=NA_HEAD_DIM)
            w_o = na_w_o[j]
        else:
            n_rot = (GQA_Q_HEADS + GQA_KV_HEADS) * GQA_HEAD_DIM
            w = gqa_w_qkv[j]
            w_rot = w[:, :n_rot].reshape(D_MODEL, GQA_Q_HEADS + GQA_KV_HEADS, GQA_HEAD_DIM)
            w_rot = _split_rotary(w_rot, GQA_HEAD_DIM).reshape(D_MODEL, n_rot)
            w = jnp.concatenate([w_rot, w[:, n_rot:]], -1).astype(BF16)
            qn = _row(_split_rotary(gqa_q_norm[j], GQA_HEAD_DIM))
            kn = _row(_split_rotary(gqa_k_norm[j], GQA_HEAD_DIM))
            q, k, v = _gqa_proj(x, mod, w, qn, kn, gqa_tables)
            qc, kc, vc = _gqa_proj(xc, mod_c, w, qn, kn, None)
            qc, kc, vc = per_batch(qc), per_batch(kc), per_batch(vc)
            dims = dict(heads=1, group=GQA_GROUP, dk=GQA_HEAD_DIM, dv=GQA_HEAD_DIM)
            y = _attention(q, kc, vc, k, v, tq=SEQ, **dims)
            if need_ctx:
                yc = _attention(qc, kc, vc, None, None, tq=CTX_LEN,
                                **{**dims, "heads": GQA_KV_HEADS})
            w_o = gqa_w_o[j]

        w_o = w_o.astype(BF16)
        g1, b1, g2, b2 = _row(ln1_g[i]), _row(ln1_b[i]), _row(ln2_g[i]), _row(ln2_b[i])
        x = _layer_tail(y, w_o, x, mod, g1, b1, w1_all, w2_all, g2, b2, i)
        if need_ctx:
            yc = yc.reshape(1, BATCH * CTX_LEN, D_MODEL)
            xc = _layer_tail(yc, w_o, xc, mod_c, g1, b1, w1_all, w2_all, g2, b2, i)
    return x
```
